```python
import jax, jax.numpy as jnp
from jax import lax
import numpy as np

D_MODEL = 4096
BATCH = 4
SEQ = 4096
DEPTH = 4

GRID_W = 64
CTX_LEN = 256
N_HEADS = 16
HEAD_K = 128
HEAD_V = 256
HK = N_HEADS * HEAD_K
HV = N_HEADS * HEAD_V
CHUNK = 64
GLA_LOWRANK = 16
GLA_GATE_NORM = 16.0
CONV_W = 5
N_EXPERTS = 16
EXPERT_FF = 256
CAPACITY_FACTOR = 2
N_GLA = (DEPTH + 1) // 2
N_GDN = DEPTH // 2
EPS = 1e-6
GLA_SPLITS = (HK, HK, HV, HV, GLA_LOWRANK, GLA_LOWRANK)
GDN_SPLITS = (2 * HK + HV, HV, N_HEADS, N_HEADS, N_HEADS, N_HEADS)
GLA_IN = sum(GLA_SPLITS)
GDN_IN = sum(GDN_SPLITS)

kernel_name = "bidir_gla_gdn_ec_moe_prefix_dit"


def _split(p, sizes):
    return jnp.split(p, [int(s) for s in np.cumsum(sizes)[:-1]], axis=-1)


def rmsnorm(x, g):
    xf = x.astype(jnp.float32)
    y = xf * lax.rsqrt(jnp.mean(xf * xf, axis=-1, keepdims=True) + EPS)
    return (y * g.astype(jnp.float32)).astype(x.dtype)


def modulate(h, shift, scale):
    return h * (1 + scale) + shift


def l2norm(u):
    return u * lax.rsqrt(jnp.sum(u * u, axis=-1, keepdims=True) + EPS)


def to_heads(u, d):
    B, L, _ = u.shape
    return u.reshape(B, L, N_HEADS, d).transpose(0, 2, 1, 3).astype(jnp.float32)


def from_heads(o):
    B, H, L, d = o.shape
    return o.transpose(0, 2, 1, 3).reshape(B, L, H * d)


def to_chunks(u):
    B, H, L = u.shape[:3]
    return u.reshape(B, H, L // CHUNK, CHUNK, *u.shape[3:])


def unchunk(o):
    o = jnp.moveaxis(o, 0, 2)
    B, H, n, C, d = o.shape
    return o.reshape(B, H, n * C, d)


def flip(u):
    return jnp.flip(u, axis=2)


def gla_chunked(q, k, v, g, s0):
    xs = tuple(jnp.moveaxis(to_chunks(t), 2, 0) for t in (q, k, v, g))
    tri = jnp.tril(jnp.ones((CHUNK, CHUNK), dtype=bool))

    def step(s, inp):
        qi, ki, vi, gi = inp
        b = jnp.cumsum(gi, axis=-2)
        diff = jnp.where(tri[:, :, None], b[..., :, None, :] - b[..., None, :, :], -jnp.inf)
        att = jnp.sum(qi[..., :, None, :] * ki[..., None, :, :] * jnp.exp(diff), axis=-1)
        o = att @ vi + (qi * jnp.exp(b)) @ s
        b_last = b[..., -1:, :]
        s = s * jnp.exp(b_last).swapaxes(-1, -2) + (ki * jnp.exp(b_last - b)).swapaxes(-1, -2) @ vi
        return s, o

    s, o = lax.scan(step, s0, xs)
    return unchunk(o), s


def gdn_chunked(q, k, v, beta, g, s0):
    q, k, v, beta, g = (to_chunks(t) for t in (q, k, v, beta, g))
    dv = v.shape[-1]
    gam = jnp.cumsum(g, axis=-1)
    tri = jnp.tril(jnp.ones((CHUNK, CHUNK), dtype=bool))
    strict = jnp.tril(jnp.ones((CHUNK, CHUNK), dtype=bool), -1)
    eye = jnp.eye(CHUNK, dtype=jnp.float32)
    decay = jnp.exp(jnp.where(tri, gam[..., :, None] - gam[..., None, :], -jnp.inf))
    kb = k * beta[..., None]
    a = jnp.where(strict, (kb @ k.swapaxes(-1, -2)) * decay, 0.0)
    rhs = jnp.concatenate([v * beta[..., None], kb * jnp.exp(gam)[..., None]], axis=-1)
    sol = lax.linalg.triangular_solve(a + eye, rhs, left_side=True, lower=True)
    u, w = sol[..., :dv], sol[..., dv:]
    qk = (q @ k.swapaxes(-1, -2)) * decay
    qg = q * jnp.exp(gam)[..., None]
    kd = k * jnp.exp(gam[..., -1:] - gam)[..., None]
    glast = jnp.exp(gam[..., -1])
    xs = tuple(jnp.moveaxis(t, 2, 0) for t in (u, w, qk, qg, kd, glast))

    def step(s, inp):
        ui, wi, qki, qgi, kdi, gli = inp
        vnew = ui - wi @ s
        o = qgi @ s + qki @ vnew
        s = s * gli[..., None, None] + kdi.swapaxes(-1, -2) @ vnew
        return s, o

    s, o = lax.scan(step, s0, xs)
    return unchunk(o), s


def head_out(o, z, g, w_out):
    y = from_heads(rmsnorm(o, g)).astype(z.dtype) * jax.nn.silu(z)
    return y @ w_out


def short_conv(u, w, grid):
    B, L, Cc = u.shape
    if grid:
        rows = L // GRID_W
        u = u.reshape(B * rows, GRID_W, Cc)
    y = lax.conv_general_dilated(u, w[:, None, :].astype(u.dtype), window_strides=(1,),
                                 padding=[(CONV_W // 2, CONV_W // 2)],
                                 dimension_numbers=('NWC', 'WIO', 'NWC'), feature_group_count=Cc)
    return y.reshape(B, L, Cc)


def gla_mixer(hc, hx, w_in, wg2, bg2, onorm_g, w_out, need_ctx_out):
    def project(h):
        q, k, v, z, lf, lb = _split(h @ w_in, GLA_SPLITS)
        gf = to_heads(jax.nn.log_sigmoid((lf @ wg2[0] + bg2[0]).astype(jnp.float32)) / GLA_GATE_NORM, HEAD_K)
        gb = to_heads(jax.nn.log_sigmoid((lb @ wg2[1] + bg2[1]).astype(jnp.float32)) / GLA_GATE_NORM, HEAD_K)
        return (to_heads(q, HEAD_K) * HEAD_K ** -0.5, to_heads(k, HEAD_K), to_heads(v, HEAD_V), gf, gb), z

    def bidir(p, sf, sb):
        q, k, v, gf, gb = p
        of, sf = gla_chunked(q, k, v, gf, sf)
        ob, sb = gla_chunked(flip(q), flip(k), flip(v), flip(gb), sb)
        return of + flip(ob), sf, sb

    pc, zc = project(hc)
    px, zx = project(hx)
    s0 = jnp.zeros((hc.shape[0], N_HEADS, HEAD_K, HEAD_V), jnp.float32)
    oc, sf, sb = bidir(pc, s0, s0)
    ox, _, _ = bidir(px, sf, sb)
    yx = head_out(ox, zx, onorm_g, w_out)
    yc = head_out(oc, zc, onorm_g, w_out) if need_ctx_out else None
    return yc, yx


def gdn_mixer(hc, hx, w_in, conv_w, A_log, dt_bias, onorm_g, w_out, need_ctx_out):
    def project(h, grid):
        qkv, z, af, ab, bf, bb = _split(h @ w_in, GDN_SPLITS)
        qkv = jax.nn.silu(short_conv(qkv, conv_w, grid))
        q, k, v = _split(qkv, (HK, HK, HV))
        q = l2norm(to_heads(q, HEAD_K)) * HEAD_K ** -0.5
        k = l2norm(to_heads(k, HEAD_K))
        v = to_heads(v, HEAD_V)

        def decay(a, d):
            sp = jax.nn.softplus(a.astype(jnp.float32) + dt_bias[d]).swapaxes(1, 2)
            return -jnp.exp(A_log[d].astype(jnp.float32))[:, None] * sp

        betaf = jax.nn.sigmoid(bf.astype(jnp.float32)).swapaxes(1, 2)
        betab = jax.nn.sigmoid(bb.astype(jnp.float32)).swapaxes(1, 2)
        return (q, k, v, betaf, decay(af, 0), betab, decay(ab, 1)), z

    def bidir(p, sf, sb):
        q, k, v, betaf, gf, betab, gb = p
        of, sf = gdn_chunked(q, k, v, betaf, gf, sf)
        ob, sb = gdn_chunked(flip(q), flip(k), flip(v), flip(betab), flip(gb), sb)
        return of + flip(ob), sf, sb

    pc, zc = project(hc, False)
    px, zx = project(hx, True)
    s0 = jnp.zeros((hc.shape[0], N_HEADS, HEAD_K, HEAD_V), jnp.float32)
    oc, sf, sb = bidir(pc, s0, s0)
    ox, _, _ = bidir(px, sf, sb)
    yx = head_out(ox, zx, onorm_g, w_out)
    yc = head_out(oc, zc, onorm_g, w_out) if need_ctx_out else None
    return yc, yx


def ec_ffn(h, w_router, w_gate, w_up, w_down):
    B, L, _ = h.shape
    cap = CAPACITY_FACTOR * L // N_EXPERTS
    aff = jax.nn.softmax((h @ w_router).astype(jnp.float32), axis=-1)
    gate, idx = lax.top_k(aff.swapaxes(1, 2), cap)
    bidx = jnp.arange(B)[:, None, None]
    xs = h[bidx, idx]
    hid = jax.nn.silu(jnp.einsum('becd,edf->becf', xs, w_gate)) * jnp.einsum('becd,edf->becf', xs, w_up)
    y = jnp.einsum('becf,efd->becd', hid, w_down) * gate[..., None].astype(h.dtype)
    return jnp.zeros_like(h).at[bidx, idx].add(y)


def setup_inputs(seed: int = 0) -> dict:
    key = jax.random.key(seed)
    ks = jax.random.split(key, 32)
    D = D_MODEL
    f32 = jnp.float32

    def nrm(k, shape, scale):
        return jax.random.normal(k, shape, f32) * scale

    dt = jnp.exp(jax.random.uniform(ks[17], (N_GDN, 2, N_HEADS), f32, np.log(1e-3), np.log(1e-1)))
    return {
        "x": nrm(ks[0], (BATCH, SEQ, D), 1.0),
        "c": nrm(ks[1], (BATCH, D), 1.0),
        "ctx": nrm(ks[2], (BATCH, CTX_LEN, D), 1.0),
        "c_ctx": nrm(ks[3], (D,), 1.0),
        "ada_w": nrm(ks[4], (DEPTH, D, 6 * D), 0.5 * D ** -0.5),
        "ada_b": nrm(ks[5], (DEPTH, 6 * D), 0.02),
        "norm1_g": 1.0 + nrm(ks[6], (DEPTH, D), 0.02),
        "norm2_g": 1.0 + nrm(ks[7], (DEPTH, D), 0.02),
        "gla_w_in": nrm(ks[8], (N_GLA, D, GLA_IN), D ** -0.5),
        "gla_wg2": nrm(ks[9], (N_GLA, 2, GLA_LOWRANK, HK), GLA_LOWRANK ** -0.5),
        "gla_bg2": nrm(ks[10], (N_GLA, 2, HK), 0.1),
        "gla_onorm": 1.0 + nrm(ks[11], (N_GLA, HEAD_V), 0.02),
        "gla_w_out": nrm(ks[12], (N_GLA, HV, D), HV ** -0.5),
        "gdn_w_in": nrm(ks[13], (N_GDN, D, GDN_IN), D ** -0.5),
        "gdn_conv": nrm(ks[14], (N_GDN, CONV_W, 2 * HK + HV), CONV_W ** -0.5),
        "gdn_A_log": jnp.log(jax.random.uniform(ks[15], (N_GDN, 2, N_HEADS), f32, 1.0, 16.0)),
        "gdn_dt_bias": dt + jnp.log(-jnp.expm1(-dt)),
        "gdn_onorm": 1.0 + nrm(ks[18], (N_GDN, HEAD_V), 0.02),
        "gdn_w_out": nrm(ks[19], (N_GDN, HV, D), HV ** -0.5),
        "router_w": nrm(ks[20], (DEPTH, D, N_EXPERTS), D ** -0.5),
        "exp_w_gate": nrm(ks[21], (DEPTH, N_EXPERTS, D, EXPERT_FF), D ** -0.5),
        "exp_w_up": nrm(ks[22], (DEPTH, N_EXPERTS, D, EXPERT_FF), D ** -0.5),
        "exp_w_down": nrm(ks[23], (DEPTH, N_EXPERTS, EXPERT_FF, D), EXPERT_FF ** -0.5),
        "final_g": 1.0 + nrm(ks[24], (D,), 0.02),
    }


def reference(x, c, ctx, c_ctx, ada_w, ada_b, norm1_g, norm2_g, gla_w_in, gla_wg2, gla_bg2, gla_onorm,
              gla_w_out, gdn_w_in, gdn_conv, gdn_A_log, gdn_dt_bias, gdn_onorm, gdn_w_out, router_w,
              exp_w_gate, exp_w_up, exp_w_down, final_g):
    cx = ctx
    for i in range(DEPTH):
        last = i == DEPTH - 1
        sh1, sc1, g1, sh2, sc2, g2 = jnp.split(jax.nn.silu(c) @ ada_w[i] + ada_b[i], 6, axis=-1)
        csh1, csc1, cg1, csh2, csc2, cg2 = jnp.split(jax.nn.silu(c_ctx) @ ada_w[i] + ada_b[i], 6, axis=-1)
        hx = modulate(rmsnorm(x, norm1_g[i]), sh1[:, None], sc1[:, None])
        hc = modulate(rmsnorm(cx, norm1_g[i]), csh1, csc1)
        j = i // 2
        if i % 2 == 0:
            yc, yx = gla_mixer(hc, hx, gla_w_in[j], gla_wg2[j], gla_bg2[j], gla_onorm[j], gla_w_out[j], not last)
        else:
            yc, yx = gdn_mixer(hc, hx, gdn_w_in[j], gdn_conv[j], gdn_A_log[j], gdn_dt_bias[j], gdn_onorm[j],
                               gdn_w_out[j], not last)
        x = x + g1[:, None] * yx
        hx2 = modulate(rmsnorm(x, norm2_g[i]), sh2[:, None], sc2[:, None])
        x = x + g2[:, None] * ec_ffn(hx2, router_w[i], exp_w_gate[i], exp_w_up[i], exp_w_down[i])
        if not last:
            cx = cx + cg1 * yc
            hc2 = modulate(rmsnorm(cx, norm2_g[i]), csh2, csc2)
            cx = cx + cg2 * ec_ffn(hc2, router_w[i], exp_w_gate[i], exp_w_up[i], exp_w_down[i])
    return rmsnorm(x, final_g)
```

```python
import functools

import jax
import jax.numpy as jnp
from jax import lax
from jax.experimental import pallas as pl
from jax.experimental.pallas import tpu as pltpu

F32 = jnp.float32
BF16 = jnp.bfloat16
I32 = jnp.int32

N_HEADS = 16
HEAD_K = 128
HEAD_V = 256
HK = N_HEADS * HEAD_K
HV = N_HEADS * HEAD_V
CHUNK = 64
GRID_W = 64
GLA_LOWRANK = 16
GLA_GATE_NORM = 16.0
CONV_W = 5
N_EXPERTS = 16
EXPERT_FF = 256
CAPACITY_FACTOR = 2
EPS = 1e-6
N_MAIN = 2 * HK + 2 * HV
N_SMALL = 128
LANES = 128
SUBLANES = 8
VMEM_LIMIT_BYTES = 56 * 1024 * 1024
Q_SCALE = HEAD_K ** -0.5


def _cparams(*sem):
    return pltpu.CompilerParams(dimension_semantics=sem, vmem_limit_bytes=VMEM_LIMIT_BYTES)


def _dot(a, b):
    return jnp.dot(a, b, preferred_element_type=F32)


def _dot_nt(a, b):
    return lax.dot_general(a, b, (((1,), (1,)), ((), ())), preferred_element_type=F32)


def _silu(x):
    return x * jax.nn.sigmoid(x)


def _log_sigmoid(x):
    return jnp.minimum(x, 0.0) - jnp.log1p(jnp.exp(-jnp.abs(x)))


def _softplus(x):
    return jnp.maximum(x, 0.0) + jnp.log1p(jnp.exp(-jnp.abs(x)))


def _split3(x):
    hi = x.astype(BF16)
    r = x - hi.astype(F32)
    mid = r.astype(BF16)
    lo = (r - mid.astype(F32)).astype(BF16)
    return hi, mid, lo


def _dot_exact_lhs01(m01, x):
    hi, mid, lo = _split3(x)
    return _dot(m01, hi) + _dot(m01, mid) + _dot(m01, lo)


def _dot_exact_rhs01(x, m01):
    hi, mid, lo = _split3(x)
    return _dot(hi, m01) + _dot(mid, m01) + _dot(lo, m01)


def _transpose_chunk(x):
    pad = jnp.zeros((LANES - CHUNK, x.shape[1]), F32)
    return jnp.concatenate([x, pad], axis=0).T[:, :CHUNK]


def _chunk_masks():
    r = lax.broadcasted_iota(I32, (CHUNK, CHUNK), 0)
    c = lax.broadcasted_iota(I32, (CHUNK, CHUNK), 1)
    return r, c


def _mod_kernel(c_ref, w_ref, b_ref, o_ref, acc_ref):
    k = pl.program_id(2)

    @pl.when(k == 0)
    def _():
        acc_ref[...] = jnp.zeros_like(acc_ref)

    acc_ref[...] += _dot(_silu(c_ref[...]).astype(BF16), w_ref[0].astype(BF16))

    @pl.when(k == pl.num_programs(2) - 1)
    def _():
        o_ref[0] = acc_ref[...] + b_ref[0]


def _modulation(c_rows, ada_w, ada_b):
    depth, d, n = ada_w.shape
    rows = c_rows.shape[0]
    tk = min(1024, d)
    tn = min(2048, n)
    return pl.pallas_call(
        _mod_kernel,
        grid=(depth, n // tn, d // tk),
        in_specs=[
            pl.BlockSpec((rows, tk), lambda l, j, k: (0, k)),
            pl.BlockSpec((1, tk, tn), lambda l, j, k: (l, k, j)),
            pl.BlockSpec((1, 1, tn), lambda l, j, k: (l, 0, j)),
        ],
        out_specs=pl.BlockSpec((1, rows, tn), lambda l, j, k: (l, 0, j)),
        out_shape=jax.ShapeDtypeStruct((depth, rows, n), F32),
        scratch_shapes=[pltpu.VMEM((rows, tn), F32)],
        compiler_params=_cparams("parallel", "parallel", "arbitrary"),
        name="adaln_modulation",
    )(c_rows, ada_w, ada_b.reshape(depth, 1, n))


def _norm_mod_rows(x, gv, scale1, shift):
    ms = jnp.mean(x * x, axis=-1, keepdims=True)
    return (x * lax.rsqrt(ms + EPS) * gv) * scale1 + shift


def _row_slab(tm):
    return 64 if tm % 64 == 0 else tm


def _inproj_kernel(x_ref, g_ref, sh_ref, sc_ref, w_ref, ws_ref, o_ref, os_ref, h_ref):
    tm = h_ref.shape[0]
    slab = _row_slab(tm)

    @pl.when(pl.program_id(2) == 0)
    def _():
        gv = g_ref[...]
        scale1 = 1.0 + sc_ref[0]
        shift = sh_ref[0]

        def body(r, carry):
            rows = pl.ds(pl.multiple_of(r * slab, slab), slab)
            h_ref[rows, :] = _norm_mod_rows(x_ref[0, rows, :], gv, scale1, shift).astype(BF16)
            return carry

        lax.fori_loop(0, tm // slab, body, 0)
        os_ref[0] = _dot(h_ref[...], ws_ref[...])

    o_ref[0] = _dot(h_ref[...], w_ref[...]).astype(o_ref.dtype)


def _inproj(x, norm_g, mod_l, mod_row, w_main, w_small):
    bsz, seq, d = x.shape
    n = w_main.shape[1]
    tm = min(512, seq)
    tn = min(1024, n)
    return pl.pallas_call(
        _inproj_kernel,
        grid=(bsz, seq // tm, n // tn),
        in_specs=[
            pl.BlockSpec((1, tm, d), lambda b, i, j: (b, i, 0)),
            pl.BlockSpec((1, d), lambda b, i, j: (0, 0)),
            pl.BlockSpec((1, 1, d), lambda b, i, j: (mod_row(b), 0, 0)),
            pl.BlockSpec((1, 1, d), lambda b, i, j: (mod_row(b), 0, 1)),
            pl.BlockSpec((d, tn), lambda b, i, j: (0, j)),
            pl.BlockSpec((d, N_SMALL), lambda b, i, j: (0, 0)),
        ],
        out_specs=[
            pl.BlockSpec((1, tm, tn), lambda b, i, j: (b, i, j)),
            pl.BlockSpec((1, tm, N_SMALL), lambda b, i, j: (b, i, 0)),
        ],
        out_shape=[
            jax.ShapeDtypeStruct((bsz, seq, n), BF16),
            jax.ShapeDtypeStruct((bsz, seq, N_SMALL), F32),
        ],
        scratch_shapes=[pltpu.VMEM((tm, d), BF16)],
        compiler_params=_cparams("parallel", "parallel", "arbitrary"),
        name="norm_inproj",
    )(x, norm_g.reshape(1, d), mod_l, mod_l, w_main, w_small)


def _gla_chunk(q_ref, k_ref, v_ref, g_ref, s_ref, o_ref, r0, tri, mask, last_row, first_touch):
    rows = pl.ds(r0, CHUNK)
    q = q_ref[0, rows, :].astype(F32) * Q_SCALE
    k = k_ref[0, rows, :].astype(F32)
    vb = v_ref[0, rows, :]
    b = _dot_exact_lhs01(tri, g_ref[rows, :])
    btot = b[last_row:last_row + 1, :]
    ref = b[CHUNK // 2:CHUNK // 2 + 1, :]
    qt = (q * jnp.exp(b - ref)).astype(BF16)
    kt = (k * jnp.exp(ref - b)).astype(BF16)
    att = jnp.where(mask, _dot_nt(qt, kt), 0.0).astype(BF16)
    s = s_ref[...]
    o = _dot(att, vb) + _dot((q * jnp.exp(b)).astype(BF16), s.astype(BF16))
    if first_touch:
        o_ref[rows, :] = o
    else:
        o_ref[rows, :] += o
    kd = k * jnp.exp(btot - b)
    stacked = jnp.concatenate([kd, jnp.broadcast_to(jnp.exp(btot), (LANES - CHUNK, HEAD_K))], axis=0)
    st = stacked.T
    s_ref[...] = s * st[:, CHUNK:CHUNK + 1] + _dot(st[:, :CHUNK].astype(BF16), vb)


def _bidir_scan(n_chunks, fwd_step, bwd_step):
    half = n_chunks // 2

    def make(first_touch):
        def body(i, carry):
            fwd_step(pl.multiple_of(i * CHUNK, CHUNK), first_touch)
            bwd_step(pl.multiple_of((n_chunks - 1 - i) * CHUNK, CHUNK), first_touch)
            return carry
        return body

    lax.fori_loop(0, half, make(True), 0)
    lax.fori_loop(half, n_chunks, make(False), 0)


def _head_finalize(o_scr, z_ref, on_ref, y_ref, seq):
    blk = min(256, seq)
    onv = on_ref[...]

    def body(r, carry):
        rows = pl.ds(pl.multiple_of(r * blk, blk), blk)
        o = o_scr[rows, :]
        ms = jnp.mean(o * o, axis=-1, keepdims=True)
        y = o * lax.rsqrt(ms + EPS) * onv
        z = z_ref[0, rows, :].astype(F32)
        y_ref[0, rows, :] = (y * _silu(z)).astype(y_ref.dtype)
        return carry

    lax.fori_loop(0, seq // blk, body, 0)


def _gla_kernel(qc, kc, vc, zc, smc, qx, kx, vx, zx, smx, wg_ref, bg_ref, on_ref,
                yc_ref, yx_ref, gfc, gbc, gfx, gbx, oc_scr, ox_scr, sf_ref, sb_ref):
    lc = qc.shape[1]
    lx = qx.shape[1]

    def gates(sm_ref, gf_ref, gb_ref, seq):
        blk = min(256, seq)

        def body(r, carry):
            rows = pl.ds(pl.multiple_of(r * blk, blk), blk)
            s = sm_ref[0, rows, :].astype(BF16)
            gf_ref[rows, :] = _log_sigmoid(_dot(s, wg_ref[0]) + bg_ref[0]) * (1.0 / GLA_GATE_NORM)
            gb_ref[rows, :] = _log_sigmoid(_dot(s, wg_ref[1]) + bg_ref[1]) * (1.0 / GLA_GATE_NORM)
            return carry

        lax.fori_loop(0, seq // blk, body, 0)

    gates(smc, gfc, gbc, lc)
    gates(smx, gfx, gbx, lx)

    r, c = _chunk_masks()
    lower = c <= r
    upper = c >= r
    tri_f = lower.astype(BF16)
    tri_b = upper.astype(BF16)

    sf_ref[...] = jnp.zeros_like(sf_ref)
    sb_ref[...] = jnp.zeros_like(sb_ref)

    def run(q_ref, k_ref, v_ref, gf_ref, gb_ref, o_scr, seq):
        def fwd(r0, first):
            _gla_chunk(q_ref, k_ref, v_ref, gf_ref, sf_ref, o_scr, r0, tri_f, lower, CHUNK - 1, first)

        def bwd(r0, first):
            _gla_chunk(q_ref, k_ref, v_ref, gb_ref, sb_ref, o_scr, r0, tri_b, upper, 0, first)

        _bidir_scan(seq // CHUNK, fwd, bwd)

    run(qc, kc, vc, gfc, gbc, oc_scr, lc)
    run(qx, kx, vx, gfx, gbx, ox_scr, lx)
    _head_finalize(oc_scr, zc, on_ref, yc_ref, lc)
    _head_finalize(ox_scr, zx, on_ref, yx_ref, lx)


def _head_specs(seq):
    kblk = HK // HEAD_K
    vblk = 2 * HK // HEAD_V
    zblk = (2 * HK + HV) // HEAD_V
    return [
        pl.BlockSpec((1, seq, HEAD_K), lambda b, h: (b, 0, h)),
        pl.BlockSpec((1, seq, HEAD_K), lambda b, h: (b, 0, kblk + h)),
        pl.BlockSpec((1, seq, HEAD_V), lambda b, h: (b, 0, vblk + h)),
        pl.BlockSpec((1, seq, HEAD_V), lambda b, h: (b, 0, zblk + h)),
        pl.BlockSpec((1, seq, N_SMALL), lambda b, h: (b, 0, 0)),
    ]


def _gla_mix(pc, smc, px, smx, wg_pad, bg, onorm):
    bsz, lc, _ = pc.shape
    lx = px.shape[1]
    assert (lc // CHUNK) % 2 == 0 and (lx // CHUNK) % 2 == 0
    return pl.pallas_call(
        _gla_kernel,
        grid=(bsz, N_HEADS),
        in_specs=_head_specs(lc) + _head_specs(lx) + [
            pl.BlockSpec((2, N_SMALL, HEAD_K), lambda b, h: (0, 0, h)),
            pl.BlockSpec((2, 1, HEAD_K), lambda b, h: (0, 0, h)),
            pl.BlockSpec((1, HEAD_V), lambda b, h: (0, 0)),
        ],
        out_specs=[
            pl.BlockSpec((1, lc, HEAD_V), lambda b, h: (b, 0, h)),
            pl.BlockSpec((1, lx, HEAD_V), lambda b, h: (b, 0, h)),
        ],
        out_shape=[
            jax.ShapeDtypeStruct((bsz, lc, HV), BF16),
            jax.ShapeDtypeStruct((bsz, lx, HV), BF16),
        ],
        scratch_shapes=[
            pltpu.VMEM((lc, HEAD_K), F32), pltpu.VMEM((lc, HEAD_K), F32),
            pltpu.VMEM((lx, HEAD_K), F32), pltpu.VMEM((lx, HEAD_K), F32),
            pltpu.VMEM((lc, HEAD_V), F32), pltpu.VMEM((lx, HEAD_V), F32),
            pltpu.VMEM((HEAD_K, HEAD_V), F32), pltpu.VMEM((HEAD_K, HEAD_V), F32),
        ],
        compiler_params=_cparams("parallel", "parallel"),
        name="gla_bidir",
    )(pc, pc, pc, pc, smc, px, px, px, px, smx, wg_pad, bg, onorm.reshape(1, HEAD_V))


def _conv_silu(u, w_ref, rowlen):
    n = u.shape[0]
    t = lax.broadcasted_iota(I32, (n, 1), 0) % rowlen
    acc = u * w_ref[CONV_W // 2:CONV_W // 2 + 1, :]
    for j in range(CONV_W):
        d = j - CONV_W // 2
        if d == 0:
            continue
        shifted = pltpu.roll(u, (-d) % n, 0)
        valid = (t + d >= 0) & (t + d < rowlen)
        acc = acc + jnp.where(valid, shifted, 0.0) * w_ref[j:j + 1, :]
    return _silu(acc)


def _l2norm(u):
    return u * lax.rsqrt(jnp.sum(u * u, axis=-1, keepdims=True) + EPS)


def _gdn_chunk(qn, kn, vv, par, s_ref, o_ref, r0, tri, incl, strict, eye, lane, last_row, first_touch):
    rows = pl.ds(r0, CHUNK)
    q = qn[rows, :]
    k = kn[rows, :]
    v = vv[rows, :]
    p = par[rows, :]
    cum = _dot_exact_lhs01(tri, p)
    gam = cum[:, lane:lane + 1]
    beta = p[:, 2 + lane:3 + lane]
    gl = gam[last_row:last_row + 1, :]
    gam_c = jnp.broadcast_to(gam, (CHUNK, CHUNK))
    gam_r = jnp.sum(jnp.where(eye, gam_c, 0.0), axis=0, keepdims=True)
    decay = jnp.exp(jnp.where(incl, gam_c - gam_r, -jnp.inf))
    kb = k * beta
    kbf = k.astype(BF16)
    a = jnp.where(strict, _dot_nt(kb.astype(BF16), kbf) * decay, 0.0)
    ab = a.astype(BF16)
    x = jnp.where(eye, 1.0, 0.0) - a
    pw = _dot(ab, ab)
    n_doublings = 5
    for it in range(n_doublings):
        pb = pw.astype(BF16)
        x = x + _dot(x.astype(BF16), pb)
        if it + 1 < n_doublings:
            pw = _dot(pb, pb)
    eg = jnp.exp(gam)
    rhs = jnp.concatenate([v * beta, kb * eg], axis=1).astype(BF16)
    sol = _dot(x.astype(BF16), rhs)
    u = sol[:, :HEAD_V]
    w = sol[:, HEAD_V:]
    qk = (_dot_nt(q.astype(BF16), kbf) * decay).astype(BF16)
    s = s_ref[...]
    sb = s.astype(BF16)
    vnew = u - _dot(w.astype(BF16), sb)
    vnb = vnew.astype(BF16)
    o = _dot((q * eg).astype(BF16), sb) + _dot(qk, vnb)
    if first_touch:
        o_ref[rows, :] = o
    else:
        o_ref[rows, :] += o
    kdt = _transpose_chunk(k * jnp.exp(gl - gam)).astype(BF16)
    s_ref[...] = s * jnp.exp(gl) + _dot(kdt, vnb)


def _gdn_kernel(alog_ref, dt_ref, qc, kc, vc, zc, smc, qx, kx, vx, zx, smx, wq_ref, wk_ref, wv_ref, on_ref,
                yc_ref, yx_ref, qnc, knc, vvc, parc, qnx, knx, vvx, parx, oc_scr, ox_scr, sf_ref, sb_ref):
    lc = qc.shape[1]
    lx = qx.shape[1]
    h = pl.program_id(1)

    row = lax.broadcasted_iota(I32, (N_SMALL, LANES), 0)
    lane = lax.broadcasted_iota(I32, (N_SMALL, LANES), 1)
    pick = ((row == h + N_HEADS * lane) & (lane < 4)).astype(BF16)
    lane1 = lax.broadcasted_iota(I32, (1, LANES), 1)
    alog = jnp.where(lane1 == 0, alog_ref[0, h], alog_ref[1, h])
    dtb = jnp.where(lane1 == 0, dt_ref[0, h], dt_ref[1, h])
    neg_a = -jnp.exp(alog)

    def prep(q_ref, k_ref, v_ref, sm_ref, qn, kn, vv, par, seq, rowlen):
        blk = rowlen if rowlen > CHUNK else min(256, seq)

        def body(r, carry):
            rows = pl.ds(pl.multiple_of(r * blk, blk), blk)
            qn[rows, :] = _l2norm(_conv_silu(q_ref[0, rows, :].astype(F32), wq_ref, rowlen)) * Q_SCALE
            kn[rows, :] = _l2norm(_conv_silu(k_ref[0, rows, :].astype(F32), wk_ref, rowlen))
            vv[rows, :] = _conv_silu(v_ref[0, rows, :].astype(F32), wv_ref, rowlen)
            picked = _dot_exact_rhs01(sm_ref[0, rows, :], pick)
            par[rows, :] = jnp.where(lane1 < 2, neg_a * _softplus(picked + dtb), jax.nn.sigmoid(picked))
            return carry

        lax.fori_loop(0, seq // blk, body, 0)

    prep(qc, kc, vc, smc, qnc, knc, vvc, parc, lc, lc)
    prep(qx, kx, vx, smx, qnx, knx, vvx, parx, lx, GRID_W)

    r, c = _chunk_masks()
    lower = c <= r
    upper = c >= r
    eye = c == r
    tri_f = lower.astype(BF16)
    tri_b = upper.astype(BF16)

    sf_ref[...] = jnp.zeros_like(sf_ref)
    sb_ref[...] = jnp.zeros_like(sb_ref)

    def run(qn, kn, vv, par, o_scr, seq):
        def fwd(r0, first):
            _gdn_chunk(qn, kn, vv, par, sf_ref, o_scr, r0, tri_f, lower, c < r, eye, 0, CHUNK - 1, first)

        def bwd(r0, first):
            _gdn_chunk(qn, kn, vv, par, sb_ref, o_scr, r0, tri_b, upper, c > r, eye, 1, 0, first)

        _bidir_scan(seq // CHUNK, fwd, bwd)

    run(qnc, knc, vvc, parc, oc_scr, lc)
    run(qnx, knx, vvx, parx, ox_scr, lx)
    _head_finalize(oc_scr, zc, on_ref, yc_ref, lc)
    _head_finalize(ox_scr, zx, on_ref, yx_ref, lx)


def _gdn_mix(pc, smc, px, smx, conv_w, a_log, dt_bias, onorm):
    bsz, lc, _ = pc.shape
    lx = px.shape[1]
    assert (lc // CHUNK) % 2 == 0 and (lx // CHUNK) % 2 == 0 and lx % GRID_W == 0
    kblk = HK // HEAD_K
    vblk = 2 * HK // HEAD_V
    smem = pl.BlockSpec(memory_space=pltpu.SMEM)
    return pl.pallas_call(
        _gdn_kernel,
        grid=(bsz, N_HEADS),
        in_specs=[smem, smem] + _head_specs(lc) + _head_specs(lx) + [
            pl.BlockSpec((CONV_W, HEAD_K), lambda b, h: (0, h)),
            pl.BlockSpec((CONV_W, HEAD_K), lambda b, h: (0, kblk + h)),
            pl.BlockSpec((CONV_W, HEAD_V), lambda b, h: (0, vblk + h)),
            pl.BlockSpec((1, HEAD_V), lambda b, h: (0, 0)),
        ],
        out_specs=[
            pl.BlockSpec((1, lc, HEAD_V), lambda b, h: (b, 0, h)),
            pl.BlockSpec((1, lx, HEAD_V), lambda b, h: (b, 0, h)),
        ],
        out_shape=[
            jax.ShapeDtypeStruct((bsz, lc, HV), BF16),
            jax.ShapeDtypeStruct((bsz, lx, HV), BF16),
        ],
        scratch_shapes=[
            pltpu.VMEM((lc, HEAD_K), F32), pltpu.VMEM((lc, HEAD_K), F32),
            pltpu.VMEM((lc, HEAD_V), F32), pltpu.VMEM((lc, LANES), F32),
            pltpu.VMEM((lx, HEAD_K), F32), pltpu.VMEM((lx, HEAD_K), F32),
            pltpu.VMEM((lx, HEAD_V), F32), pltpu.VMEM((lx, LANES), F32),
            pltpu.VMEM((lc, HEAD_V), F32), pltpu.VMEM((lx, HEAD_V), F32),
            pltpu.VMEM((HEAD_K, HEAD_V), F32), pltpu.VMEM((HEAD_K, HEAD_V), F32),
        ],
        compiler_params=_cparams("parallel", "parallel"),
        name="gdn_bidir",
    )(a_log, dt_bias, pc, pc, pc, pc, smc, px, px, px, px, smx, conv_w, conv_w, conv_w,
      onorm.reshape(1, HEAD_V))


def _outproj_kernel(y_ref, w_ref, x_ref, g_ref, o_ref):
    o_ref[0] = x_ref[0] + g_ref[0] * _dot(y_ref[0], w_ref[...])


def _outproj(y, w, x, mod_l, mod_row, gate_blk):
    bsz, seq, d = x.shape
    kdim = y.shape[2]
    tm = min(512, seq)
    tn = min(1024, d)
    nblk = d // tn
    return pl.pallas_call(
        _outproj_kernel,
        grid=(bsz, seq // tm, nblk),
        in_specs=[
            pl.BlockSpec((1, tm, kdim), lambda b, i, j: (b, i, 0)),
            pl.BlockSpec((kdim, tn), lambda b, i, j: (0, j)),
            pl.BlockSpec((1, tm, tn), lambda b, i, j: (b, i, j)),
            pl.BlockSpec((1, 1, tn), lambda b, i, j: (mod_row(b), 0, gate_blk * nblk + j)),
        ],
        out_specs=pl.BlockSpec((1, tm, tn), lambda b, i, j: (b, i, j)),
        out_shape=jax.ShapeDtypeStruct(x.shape, F32),
        compiler_params=_cparams("parallel", "parallel", "parallel"),
        name="outproj_residual",
    )(y, w, x, mod_l)


def _router_kernel(x_ref, g_ref, sh_ref, sc_ref, wr_ref, a_ref, hi_ref, lo_ref):
    tm = hi_ref.shape[0]
    slab = _row_slab(tm)
    gv = g_ref[...]
    scale1 = 1.0 + sc_ref[0]
    shift = sh_ref[0]

    def body(r, carry):
        rows = pl.ds(pl.multiple_of(r * slab, slab), slab)
        h = _norm_mod_rows(x_ref[0, rows, :], gv, scale1, shift)
        hi = h.astype(BF16)
        hi_ref[rows, :] = hi
        lo_ref[rows, :] = (h - hi.astype(F32)).astype(BF16)
        return carry

    lax.fori_loop(0, tm // slab, body, 0)
    w = wr_ref[...]
    w_hi = w.astype(BF16)
    w_lo = (w - w_hi.astype(F32)).astype(BF16)
    hi = hi_ref[...]
    logits = _dot_nt(w_hi, hi) + _dot_nt(w_lo, hi) + _dot_nt(w_hi, lo_ref[...])
    m = jnp.max(logits, axis=0, keepdims=True)
    e = jnp.exp(logits - m)
    a_ref[0] = e / jnp.sum(e, axis=0, keepdims=True)


def _router(x, norm_g, mod_l, mod_row, w_router_t):
    bsz, seq, d = x.shape
    tm = min(512, seq)
    return pl.pallas_call(
        _router_kernel,
        grid=(bsz, seq // tm),
        in_specs=[
            pl.BlockSpec((1, tm, d), lambda b, i: (b, i, 0)),
            pl.BlockSpec((1, d), lambda b, i: (0, 0)),
            pl.BlockSpec((1, 1, d), lambda b, i: (mod_row(b), 0, 3)),
            pl.BlockSpec((1, 1, d), lambda b, i: (mod_row(b), 0, 4)),
            pl.BlockSpec((N_EXPERTS, d), lambda b, i: (0, 0)),
        ],
        out_specs=pl.BlockSpec((1, N_EXPERTS, tm), lambda b, i: (b, 0, i)),
        out_shape=jax.ShapeDtypeStruct((bsz, N_EXPERTS, seq), F32),
        scratch_shapes=[pltpu.VMEM((tm, d), BF16), pltpu.VMEM((tm, d), BF16)],
        compiler_params=_cparams("parallel", "parallel"),
        name="router_affinity",
    )(x, norm_g.reshape(1, d), mod_l, mod_l, w_router_t)


def _lane_cumsum(src_ref, dst_ref, seq):
    r = lax.broadcasted_iota(I32, (LANES, LANES), 0)
    c = lax.broadcasted_iota(I32, (LANES, LANES), 1)
    upper = (r <= c).astype(BF16)
    carry = jnp.zeros((N_EXPERTS, 1), F32)
    for jb in range(seq // LANES):
        sl = slice(jb * LANES, (jb + 1) * LANES)
        cs = _dot(src_ref[:, sl].astype(BF16), upper) + carry
        dst_ref[:, sl] = cs
        carry = cs[:, LANES - 1:LANES]


def _topk_kernel(a_ref, idx_ref, tok_ref, m_scr, cum_scr, pack_scr, *, cap):
    a = a_ref[0]
    seq = a.shape[1]
    bits = lax.bitcast_convert_type(a, I32)

    def search(i, prefix):
        cand = prefix | jnp.left_shift(jnp.int32(1), 30 - i)
        cnt = jnp.sum((bits >= cand).astype(F32), axis=1, keepdims=True)
        return jnp.where(cnt >= cap, cand, prefix)

    thr = lax.fori_loop(0, 31, search, jnp.zeros((N_EXPERTS, 1), I32))
    gt = bits > thr
    eq = bits == thr
    need = cap - jnp.sum(gt.astype(F32), axis=1, keepdims=True)
    m_scr[...] = eq.astype(F32)
    _lane_cumsum(m_scr, cum_scr, seq)
    sel = gt | (eq & (cum_scr[...] - 1.0 < need))
    m_scr[...] = sel.astype(F32)
    _lane_cumsum(m_scr, cum_scr, seq)

    pack_scr[...] = jnp.zeros_like(pack_scr)
    pack_scr[0:N_EXPERTS, :] = jnp.where(sel, cum_scr[...] - 1.0, -1.0)
    pack_scr[N_EXPERTS:2 * N_EXPERTS, :] = jnp.where(sel, a, 0.0)
    tok_ref[0] = pack_scr[...].T

    cap_pad = idx_ref.shape[2]
    ones = jnp.ones((SUBLANES, LANES), BF16)
    for e in range(N_EXPERTS):
        for pb in range(cap_pad // LANES):
            slot = (lax.broadcasted_iota(I32, (LANES, 1), 0) + pb * LANES).astype(F32)

            def count_block(jb, acc, e=e, slot=slot):
                row = cum_scr[e:e + 1, pl.ds(pl.multiple_of(jb * LANES, LANES), LANES)]
                return acc + (row <= slot).astype(F32)

            acc = lax.fori_loop(0, seq // LANES, count_block, jnp.zeros((LANES, LANES), F32))
            counts = _dot_nt(ones, acc.astype(BF16))
            idx_ref[0, e:e + 1, pb * LANES:(pb + 1) * LANES] = (
                jnp.minimum(counts[0:1, :], seq - 1.0).astype(I32))


def _topk(aff_t, cap, cap_pad):
    bsz, _, seq = aff_t.shape
    return pl.pallas_call(
        functools.partial(_topk_kernel, cap=cap),
        grid=(bsz,),
        in_specs=[pl.BlockSpec((1, N_EXPERTS, seq), lambda b: (b, 0, 0))],
        out_specs=[
            pl.BlockSpec((1, N_EXPERTS, cap_pad), lambda b: (b, 0, 0)),
            pl.BlockSpec((1, seq, LANES), lambda b: (b, 0, 0)),
        ],
        out_shape=[
            jax.ShapeDtypeStruct((bsz, N_EXPERTS, cap_pad), I32),
            jax.ShapeDtypeStruct((bsz, seq, LANES), F32),
        ],
        scratch_shapes=[
            pltpu.VMEM((N_EXPERTS, seq), F32), pltpu.VMEM((N_EXPERTS, seq), F32),
            pltpu.VMEM((LANES, seq), F32),
        ],
        compiler_params=_cparams("parallel"),
        name="expert_choice_select",
    )(aff_t)


def _expert_kernel(idx_ref, x_hbm, g_ref, sh_ref, sc_ref, wg_ref, wu_ref, o_ref, xs_ref, sem, *, cap_pad):
    e = pl.program_id(0)
    b = pl.program_id(1)
    base = (b * N_EXPERTS + e) * cap_pad

    def row_copy(p, t):
        return pltpu.make_async_copy(x_hbm.at[b, pl.ds(t, 1), :], xs_ref.at[pl.ds(p, 1), :], sem)

    def issue(p, carry):
        row_copy(p, idx_ref[base + p]).start()
        return carry

    lax.fori_loop(0, cap_pad, issue, 0)

    def drain(p, carry):
        row_copy(p, 0).wait()
        return carry

    lax.fori_loop(0, cap_pad, drain, 0)

    slab = _row_slab(cap_pad)
    gv = g_ref[...]
    scale1 = 1.0 + sc_ref[0]
    shift = sh_ref[0]

    def body(r, carry):
        rows = pl.ds(pl.multiple_of(r * slab, slab), slab)
        hs = _norm_mod_rows(xs_ref[rows, :], gv, scale1, shift).astype(BF16)
        hid = _silu(_dot(hs, wg_ref[0])) * _dot(hs, wu_ref[0])
        o_ref[0, 0, rows, :] = hid.astype(o_ref.dtype)
        return carry

    lax.fori_loop(0, cap_pad // slab, body, 0)


def _experts(idx, x, norm_g, mod_l, mod_row, w_gate, w_up):
    bsz, seq, d = x.shape
    cap_pad = idx.shape[2]
    grid_spec = pltpu.PrefetchScalarGridSpec(
        num_scalar_prefetch=1,
        grid=(N_EXPERTS, bsz),
        in_specs=[
            pl.BlockSpec(memory_space=pl.ANY),
            pl.BlockSpec((1, d), lambda e, b, idx_ref: (0, 0)),
            pl.BlockSpec((1, 1, d), lambda e, b, idx_ref: (mod_row(b), 0, 3)),
            pl.BlockSpec((1, 1, d), lambda e, b, idx_ref: (mod_row(b), 0, 4)),
            pl.BlockSpec((1, d, EXPERT_FF), lambda e, b, idx_ref: (e, 0, 0)),
            pl.BlockSpec((1, d, EXPERT_FF), lambda e, b, idx_ref: (e, 0, 0)),
        ],
        out_specs=pl.BlockSpec((1, 1, cap_pad, EXPERT_FF), lambda e, b, idx_ref: (b, e, 0, 0)),
        scratch_shapes=[pltpu.VMEM((cap_pad, d), F32), pltpu.SemaphoreType.DMA(())],
    )
    return pl.pallas_call(
        functools.partial(_expert_kernel, cap_pad=cap_pad),
        grid_spec=grid_spec,
        out_shape=jax.ShapeDtypeStruct((bsz, N_EXPERTS, cap_pad, EXPERT_FF), BF16),
        compiler_params=_cparams("arbitrary", "arbitrary"),
        name="expert_gather_swiglu",
    )(idx.reshape(-1), x, norm_g.reshape(1, d), mod_l, mod_l, w_gate, w_up)


def _down_kernel(tok_ref, hid_ref, wd_ref, x_ref, g_ref, o_ref, hcat_ref):
    tm = hcat_ref.shape[0]
    cap_pad = hid_ref.shape[2]

    @pl.when(pl.program_id(2) == 0)
    def _():
        tok = tok_ref[0]
        slot = lax.broadcasted_iota(I32, (tm, cap_pad), 1).astype(F32)
        for e in range(N_EXPERTS):
            onehot = (tok[:, e:e + 1] == slot).astype(BF16)
            rows = _dot(onehot, hid_ref[0, e])
            hcat_ref[:, e * EXPERT_FF:(e + 1) * EXPERT_FF] = (
                rows * tok[:, N_EXPERTS + e:N_EXPERTS + e + 1]).astype(BF16)

    o_ref[0] = x_ref[0] + g_ref[0] * _dot(hcat_ref[...], wd_ref[...])


def _combine_down(tok, hid, w_down_cat, x, mod_l, mod_row):
    bsz, seq, d = x.shape
    cap_pad = hid.shape[2]
    kdim = N_EXPERTS * EXPERT_FF
    tm = min(512, seq)
    tn = min(1024, d)
    nblk = d // tn
    return pl.pallas_call(
        _down_kernel,
        grid=(bsz, seq // tm, nblk),
        in_specs=[
            pl.BlockSpec((1, tm, LANES), lambda b, i, j: (b, i, 0)),
            pl.BlockSpec((1, N_EXPERTS, cap_pad, EXPERT_FF), lambda b, i, j: (b, 0, 0, 0)),
            pl.BlockSpec((kdim, tn), lambda b, i, j: (0, j)),
            pl.BlockSpec((1, tm, tn), lambda b, i, j: (b, i, j)),
            pl.BlockSpec((1, 1, tn), lambda b, i, j: (mod_row(b), 0, 5 * nblk + j)),
        ],
        out_specs=pl.BlockSpec((1, tm, tn), lambda b, i, j: (b, i, j)),
        out_shape=jax.ShapeDtypeStruct(x.shape, F32),
        scratch_shapes=[pltpu.VMEM((tm, kdim), BF16)],
        compiler_params=_cparams("parallel", "parallel", "arbitrary"),
        name="combine_down_residual",
    )(tok, hid, w_down_cat, x, mod_l)


def _ec_ffn_residual(x, norm_g, mod_l, mod_row, w_router_t, w_gate, w_up, w_down_cat):
    seq = x.shape[1]
    cap = CAPACITY_FACTOR * seq // N_EXPERTS
    cap_pad = -(-cap // LANES) * LANES
    aff_t = _router(x, norm_g, mod_l, mod_row, w_router_t)
    idx, tok = _topk(aff_t, cap, cap_pad)
    hid = _experts(idx, x, norm_g, mod_l, mod_row, w_gate, w_up)
    return _combine_down(tok, hid, w_down_cat, x, mod_l, mod_row)


def _final_kernel(x_ref, g_ref, o_ref):
    tm = x_ref.shape[1]
    slab = _row_slab(tm)
    gv = g_ref[...]

    def body(r, carry):
        rows = pl.ds(pl.multiple_of(r * slab, slab), slab)
        x = x_ref[0, rows, :]
        ms = jnp.mean(x * x, axis=-1, keepdims=True)
        o_ref[0, rows, :] = x * lax.rsqrt(ms + EPS) * gv
        return carry

    lax.fori_loop(0, tm // slab, body, 0)


def _final_norm(x, g):
    bsz, seq, d = x.shape
    tm = min(512, seq)
    return pl.pallas_call(
        _final_kernel,
        grid=(bsz, seq // tm),
        in_specs=[
            pl.BlockSpec((1, tm, d), lambda b, i: (b, i, 0)),
            pl.BlockSpec((1, d), lambda b, i: (0, 0)),
        ],
        out_specs=pl.BlockSpec((1, tm, d), lambda b, i: (b, i, 0)),
        out_shape=jax.ShapeDtypeStruct(x.shape, F32),
        compiler_params=_cparams("parallel", "parallel"),
        name="final_rmsnorm",
    )(x, g.reshape(1, d))


def _split_inproj_weight(w_in):
    w_main = w_in[:, :N_MAIN].astype(BF16)
    w_small = w_in[:, N_MAIN:]
    w_small = jnp.pad(w_small, ((0, 0), (0, N_SMALL - w_small.shape[1]))).astype(BF16)
    return w_main, w_small


def kernel(x, c, ctx, c_ctx, ada_w, ada_b, norm1_g, norm2_g, gla_w_in, gla_wg2, gla_bg2, gla_onorm, gla_w_out, gdn_w_in, gdn_conv, gdn_A_log, gdn_dt_bias, gdn_onorm, gdn_w_out, router_w, exp_w_gate, exp_w_up, exp_w_down, final_g):
    bsz, _, d = x.shape
    depth = ada_w.shape[0]
    n_rows = -(-(bsz + 1) // SUBLANES) * SUBLANES
    c_rows = jnp.zeros((n_rows, d), F32).at[:bsz].set(c).at[bsz].set(c_ctx)
    mod = _modulation(c_rows, ada_w, ada_b)

    def lat_row(b):
        return b

    def ctx_row(b):
        return bsz

    cx = ctx
    for i in range(depth):
        last = i == depth - 1
        j = i // 2
        mod_l = mod[i].reshape(n_rows, 1, 6 * d)
        if i % 2 == 0:
            w_main, w_small = _split_inproj_weight(gla_w_in[j])
            w_out = gla_w_out[j].astype(BF16)
        else:
            w_main, w_small = _split_inproj_weight(gdn_w_in[j])
            w_out = gdn_w_out[j].astype(BF16)
        pc, smc = _inproj(cx, norm1_g[i], mod_l, ctx_row, w_main, w_small)
        px, smx = _inproj(x, norm1_g[i], mod_l, lat_row, w_main, w_small)
        if i % 2 == 0:
            wg_pad = jnp.zeros((2, N_SMALL, HK), F32)
            wg_pad = wg_pad.at[0, :GLA_LOWRANK].set(gla_wg2[j, 0])
            wg_pad = wg_pad.at[1, GLA_LOWRANK:2 * GLA_LOWRANK].set(gla_wg2[j, 1]).astype(BF16)
            yc, yx = _gla_mix(pc, smc, px, smx, wg_pad, gla_bg2[j].reshape(2, 1, HK), gla_onorm[j])
        else:
            yc, yx = _gdn_mix(pc, smc, px, smx, gdn_conv[j], gdn_A_log[j], gdn_dt_bias[j], gdn_onorm[j])
        w_router_t = router_w[i].T
        w_gate = exp_w_gate[i].astype(BF16)
        w_up = exp_w_up[i].astype(BF16)
        w_down_cat = exp_w_down[i].reshape(N_EXPERTS * EXPERT_FF, d).astype(BF16)
        x = _outproj(yx, w_out, x, mod_l, lat_row, 2)
        x = _ec_ffn_residual(x, norm2_g[i], mod_l, lat_row, w_router_t, w_gate, w_up, w_down_cat)
        if not last:
            cx = _outproj(yc, w_out, cx, mod_l, ctx_row, 2)
            cx = _ec_ffn_residual(cx, norm2_g[i], mod_l, ctx_row, w_router_t, w_gate, w_up, w_down_cat)
    return _final_norm(x, final_g)
```

```python
import functools

import jax
import jax.numpy as jnp
from jax import lax
from jax.experimental import pallas as pl
from jax.experimental.pallas import tpu as pltpu

F32 = jnp.float32
BF16 = jnp.bfloat16
I32 = jnp.int32

N_HEADS = 16
HEAD_K = 128
HEAD_V = 256
HK = N_HEADS * HEAD_K
HV = N_HEADS * HEAD_V
CHUNK = 64
GRID_W = 64
GLA_LOWRANK = 16
GLA_GATE_NORM = 16.0
CONV_W = 5
N_EXPERTS = 16
EXPERT_FF = 256
CAPACITY_FACTOR = 2
EPS = 1e-6
N_MAIN = 2 * HK + 2 * HV
N_SMALL = 128
LANES = 128
SUBLANES = 8
VMEM_LIMIT_BYTES = 56 * 1024 * 1024
Q_SCALE = HEAD_K ** -0.5
K_CHUNKS = 4


def _cparams(*sem):
    return pltpu.CompilerParams(dimension_semantics=sem, vmem_limit_bytes=VMEM_LIMIT_BYTES)


def _dot(a, b):
    return jnp.dot(a, b, preferred_element_type=F32)


def _dot_nt(a, b):
    return lax.dot_general(a, b, (((1,), (1,)), ((), ())), preferred_element_type=F32)


def _silu(x):
    return x * jax.nn.sigmoid(x)


def _log_sigmoid(x):
    return jnp.minimum(x, 0.0) - jnp.log1p(jnp.exp(-jnp.abs(x)))


def _softplus(x):
    return jnp.maximum(x, 0.0) + jnp.log1p(jnp.exp(-jnp.abs(x)))


def _split3(x):
    hi = x.astype(BF16)
    r = x - hi.astype(F32)
    mid = r.astype(BF16)
    lo = (r - mid.astype(F32)).astype(BF16)
    return hi, mid, lo


def _dot_exact_rhs01(x, m01):
    hi, mid, lo = _split3(x)
    return _dot(hi, m01) + _dot(mid, m01) + _dot(lo, m01)


def _chunk_cumsum(x, reverse):
    n = x.shape[0]
    t = lax.broadcasted_iota(I32, (n, 1), 0) % CHUNK
    s = 1
    while s < CHUNK:
        if reverse:
            shifted = pltpu.roll(x, n - s, 0)
            valid = t < CHUNK - s
        else:
            shifted = pltpu.roll(x, s, 0)
            valid = t >= s
        x = x + jnp.where(valid, shifted, 0.0)
        s *= 2
    return x


def _transpose_chunk(x):
    pad = jnp.zeros((LANES - CHUNK, x.shape[1]), F32)
    return jnp.concatenate([x, pad], axis=0).T[:, :CHUNK]


def _chunk_masks():
    r = lax.broadcasted_iota(I32, (CHUNK, CHUNK), 0)
    c = lax.broadcasted_iota(I32, (CHUNK, CHUNK), 1)
    return r, c


def _mod_kernel(c_ref, w_ref, b_ref, o_ref, acc_ref):
    k = pl.program_id(2)

    @pl.when(k == 0)
    def _():
        acc_ref[...] = jnp.zeros_like(acc_ref)

    acc_ref[...] += _dot(_silu(c_ref[...]).astype(BF16), w_ref[0].astype(BF16))

    @pl.when(k == pl.num_programs(2) - 1)
    def _():
        o_ref[0] = acc_ref[...] + b_ref[0]


def _modulation(c_rows, ada_w, ada_b):
    depth, d, n = ada_w.shape
    rows = c_rows.shape[0]
    tk = min(1024, d)
    tn = min(2048, n)
    return pl.pallas_call(
        _mod_kernel,
        grid=(depth, n // tn, d // tk),
        in_specs=[
            pl.BlockSpec((rows, tk), lambda l, j, k: (0, k)),
            pl.BlockSpec((1, tk, tn), lambda l, j, k: (l, k, j)),
            pl.BlockSpec((1, 1, tn), lambda l, j, k: (l, 0, j)),
        ],
        out_specs=pl.BlockSpec((1, rows, tn), lambda l, j, k: (l, 0, j)),
        out_shape=jax.ShapeDtypeStruct((depth, rows, n), F32),
        scratch_shapes=[pltpu.VMEM((rows, tn), F32)],
        compiler_params=_cparams("parallel", "parallel", "arbitrary"),
        name="adaln_modulation",
    )(c_rows, ada_w, ada_b.reshape(depth, 1, n))


def _norm_mod_rows(x, gv, scale1, shift):
    ms = jnp.mean(x * x, axis=-1, keepdims=True)
    return (x * lax.rsqrt(ms + EPS) * gv) * scale1 + shift


def _row_slab(tm):
    return 64 if tm % 64 == 0 else tm


def _inproj_kernel(x_ref, g_ref, sh_ref, sc_ref, w_ref, ws_ref, o_ref, os_ref, h_ref):
    tm = h_ref.shape[0]
    slab = _row_slab(tm)

    @pl.when(pl.program_id(2) == 0)
    def _():
        gv = g_ref[...]
        scale1 = 1.0 + sc_ref[0]
        shift = sh_ref[0]

        def body(r, carry):
            rows = pl.ds(pl.multiple_of(r * slab, slab), slab)
            h_ref[rows, :] = _norm_mod_rows(x_ref[0, rows, :], gv, scale1, shift).astype(BF16)
            return carry

        lax.fori_loop(0, tm // slab, body, 0)
        os_ref[0] = _dot(h_ref[...], ws_ref[...])

    o_ref[0] = _dot(h_ref[...], w_ref[...]).astype(o_ref.dtype)


def _inproj(x, norm_g, mod_l, mod_row, w_main, w_small):
    bsz, seq, d = x.shape
    n = w_main.shape[1]
    tm = min(512, seq)
    tn = min(1024, n)
    return pl.pallas_call(
        _inproj_kernel,
        grid=(bsz, seq // tm, n // tn),
        in_specs=[
            pl.BlockSpec((1, tm, d), lambda b, i, j: (b, i, 0)),
            pl.BlockSpec((1, d), lambda b, i, j: (0, 0)),
            pl.BlockSpec((1, 1, d), lambda b, i, j: (mod_row(b), 0, 0)),
            pl.BlockSpec((1, 1, d), lambda b, i, j: (mod_row(b), 0, 1)),
            pl.BlockSpec((d, tn), lambda b, i, j: (0, j)),
            pl.BlockSpec((d, N_SMALL), lambda b, i, j: (0, 0)),
        ],
        out_specs=[
            pl.BlockSpec((1, tm, tn), lambda b, i, j: (b, i, j)),
            pl.BlockSpec((1, tm, N_SMALL), lambda b, i, j: (b, i, 0)),
        ],
        out_shape=[
            jax.ShapeDtypeStruct((bsz, seq, n), BF16),
            jax.ShapeDtypeStruct((bsz, seq, N_SMALL), F32),
        ],
        scratch_shapes=[pltpu.VMEM((tm, d), BF16)],
        compiler_params=_cparams("parallel", "parallel", "arbitrary"),
        name="norm_inproj",
    )(x, norm_g.reshape(1, d), mod_l, mod_l, w_main, w_small)


def _bidir_blocks(n_chunks, prep_all, step_pair):
    kc = K_CHUNKS
    nblk = n_chunks // kc

    def body(i, carry):
        fbase = pl.multiple_of(i * (kc * CHUNK), kc * CHUNK)
        bbase = pl.multiple_of((nblk - 1 - i) * (kc * CHUNK), kc * CHUNK)
        prep_all([(c, fbase + c * CHUNK, 0) for c in range(kc)]
                 + [(kc + c, bbase + c * CHUNK, 1) for c in range(kc)])
        for c in range(kc):
            step_pair((c, fbase + c * CHUNK), (2 * kc - 1 - c, bbase + (kc - 1 - c) * CHUNK))
        return carry

    lax.fori_loop(0, nblk, body, 0)


def _zero_rows(ref, seq):
    blk = min(256, seq)

    def body(r, carry):
        ref[pl.ds(pl.multiple_of(r * blk, blk), blk), :] = jnp.zeros((blk, ref.shape[1]), ref.dtype)
        return carry

    lax.fori_loop(0, seq // blk, body, 0)


def _gla_prep_all(q_ref, k_ref, v_ref, b_refs, o_ref, qe_ref, st_ref, nc_ref, chains, masks):
    last_row = (CHUNK - 1, 0)
    rows = [pl.ds(r0, CHUNK) for _, r0, _ in chains]
    dirs = [d for _, _, d in chains]
    q = [q_ref[0, r, :].astype(F32) * Q_SCALE for r in rows]
    k = [k_ref[0, r, :].astype(F32) for r in rows]
    vb = [v_ref[0, r, :] for r in rows]
    b = [b_refs[d][r, :] for r, d in zip(rows, dirs)]
    btot = [bi[last_row[d]:last_row[d] + 1, :] for bi, d in zip(b, dirs)]
    ref = [bi[CHUNK // 2:CHUNK // 2 + 1, :] for bi in b]
    qt = [(qi * jnp.exp(bi - ri)).astype(BF16) for qi, bi, ri in zip(q, b, ref)]
    kt = [(ki * jnp.exp(ri - bi)).astype(BF16) for ki, bi, ri in zip(k, b, ref)]
    scores = [_dot_nt(qi, ki) for qi, ki in zip(qt, kt)]
    stacked = [jnp.concatenate([ki * jnp.exp(bt - bi), jnp.broadcast_to(jnp.exp(bt), (LANES - CHUNK, HEAD_K))],
                               axis=0) for ki, bi, bt in zip(k, b, btot)]
    st = [s.T for s in stacked]
    att = [jnp.where(masks[d], sc, 0.0).astype(BF16) for sc, d in zip(scores, dirs)]
    intra = [_dot(ai, vi) for ai, vi in zip(att, vb)]
    inc = [_dot(si[:, :CHUNK].astype(BF16), vi) for si, vi in zip(st, vb)]
    for (slot, _, _), r, qi, bi, si, ni, oi in zip(chains, rows, q, b, st, inc, intra):
        qe_ref[slot] = (qi * jnp.exp(bi)).astype(BF16)
        st_ref[slot] = si
        nc_ref[slot] = ni
        o_ref[r, :] += oi


def _gla_step_pair(o_ref, qe_ref, st_ref, nc_ref, s_refs, pair):
    s = [s_ref[...] for s_ref in s_refs]
    inter = [_dot(qe_ref[slot], si.astype(BF16)) for (slot, _), si in zip(pair, s)]
    for (slot, r0), s_ref, si, oi in zip(pair, s_refs, s, inter):
        o_ref[pl.ds(r0, CHUNK), :] += oi
        s_ref[...] = si * st_ref[slot][:, CHUNK:CHUNK + 1] + nc_ref[slot]


def _head_finalize(o_scr, z_ref, on_ref, y_ref, seq):
    blk = min(256, seq)
    onv = on_ref[...]

    def body(r, carry):
        rows = pl.ds(pl.multiple_of(r * blk, blk), blk)
        o = o_scr[rows, :]
        ms = jnp.mean(o * o, axis=-1, keepdims=True)
        y = o * lax.rsqrt(ms + EPS) * onv
        z = z_ref[0, rows, :].astype(F32)
        y_ref[0, rows, :] = (y * _silu(z)).astype(y_ref.dtype)
        return carry

    lax.fori_loop(0, seq // blk, body, 0)


def _gla_kernel(qc, kc, vc, zc, smc, qx, kx, vx, zx, smx, wg_ref, bg_ref, on_ref,
                yc_ref, yx_ref, bfc, bbc, bfx, bbx, oc_scr, ox_scr, sf_ref, sb_ref, qe_ref, st_ref, nc_ref):
    lc = qc.shape[1]
    lx = qx.shape[1]

    def gates(sm_ref, bf_ref, bb_ref, seq):
        blk = min(256, seq)

        def body(r, carry):
            rows = pl.ds(pl.multiple_of(r * blk, blk), blk)
            s = sm_ref[0, rows, :].astype(BF16)
            gf = _log_sigmoid(_dot(s, wg_ref[0]) + bg_ref[0]) * (1.0 / GLA_GATE_NORM)
            gb = _log_sigmoid(_dot(s, wg_ref[1]) + bg_ref[1]) * (1.0 / GLA_GATE_NORM)
            bf_ref[rows, :] = _chunk_cumsum(gf, False)
            bb_ref[rows, :] = _chunk_cumsum(gb, True)
            return carry

        lax.fori_loop(0, seq // blk, body, 0)

    gates(smc, bfc, bbc, lc)
    gates(smx, bfx, bbx, lx)

    r, c = _chunk_masks()
    lower = c <= r
    upper = c >= r

    sf_ref[...] = jnp.zeros_like(sf_ref)
    sb_ref[...] = jnp.zeros_like(sb_ref)
    _zero_rows(oc_scr, lc)
    _zero_rows(ox_scr, lx)

    def run(q_ref, k_ref, v_ref, bf_ref, bb_ref, o_scr, seq):
        def prep_all(chains):
            _gla_prep_all(q_ref, k_ref, v_ref, (bf_ref, bb_ref), o_scr, qe_ref, st_ref, nc_ref, chains,
                          (lower, upper))

        def step_pair(fwd, bwd):
            _gla_step_pair(o_scr, qe_ref, st_ref, nc_ref, (sf_ref, sb_ref), (fwd, bwd))

        _bidir_blocks(seq // CHUNK, prep_all, step_pair)

    run(qc, kc, vc, bfc, bbc, oc_scr, lc)
    run(qx, kx, vx, bfx, bbx, ox_scr, lx)
    _head_finalize(oc_scr, zc, on_ref, yc_ref, lc)
    _head_finalize(ox_scr, zx, on_ref, yx_ref, lx)


def _head_specs(seq):
    kblk = HK // HEAD_K
    vblk = 2 * HK // HEAD_V
    zblk = (2 * HK + HV) // HEAD_V
    return [
        pl.BlockSpec((1, seq, HEAD_K), lambda b, h: (b, 0, h)),
        pl.BlockSpec((1, seq, HEAD_K), lambda b, h: (b, 0, kblk + h)),
        pl.BlockSpec((1, seq, HEAD_V), lambda b, h: (b, 0, vblk + h)),
        pl.BlockSpec((1, seq, HEAD_V), lambda b, h: (b, 0, zblk + h)),
        pl.BlockSpec((1, seq, N_SMALL), lambda b, h: (b, 0, 0)),
    ]


def _gla_mix(pc, smc, px, smx, wg_pad, bg, onorm):
    bsz, lc, _ = pc.shape
    lx = px.shape[1]
    assert lc % (K_CHUNKS * CHUNK) == 0 and lx % (K_CHUNKS * CHUNK) == 0
    return pl.pallas_call(
        _gla_kernel,
        grid=(bsz, N_HEADS),
        in_specs=_head_specs(lc) + _head_specs(lx) + [
            pl.BlockSpec((2, N_SMALL, HEAD_K), lambda b, h: (0, 0, h)),
            pl.BlockSpec((2, 1, HEAD_K), lambda b, h: (0, 0, h)),
            pl.BlockSpec((1, HEAD_V), lambda b, h: (0, 0)),
        ],
        out_specs=[
            pl.BlockSpec((1, lc, HEAD_V), lambda b, h: (b, 0, h)),
            pl.BlockSpec((1, lx, HEAD_V), lambda b, h: (b, 0, h)),
        ],
        out_shape=[
            jax.ShapeDtypeStruct((bsz, lc, HV), BF16),
            jax.ShapeDtypeStruct((bsz, lx, HV), BF16),
        ],
        scratch_shapes=[
            pltpu.VMEM((lc, HEAD_K), F32), pltpu.VMEM((lc, HEAD_K), F32),
            pltpu.VMEM((lx, HEAD_K), F32), pltpu.VMEM((lx, HEAD_K), F32),
            pltpu.VMEM((lc, HEAD_V), F32), pltpu.VMEM((lx, HEAD_V), F32),
            pltpu.VMEM((HEAD_K, HEAD_V), F32), pltpu.VMEM((HEAD_K, HEAD_V), F32),
            pltpu.VMEM((2 * K_CHUNKS, CHUNK, HEAD_K), BF16),
            pltpu.VMEM((2 * K_CHUNKS, HEAD_K, LANES), F32),
            pltpu.VMEM((2 * K_CHUNKS, HEAD_K, HEAD_V), F32),
        ],
        compiler_params=_cparams("parallel", "parallel"),
        name="gla_bidir",
    )(pc, pc, pc, pc, smc, px, px, px, px, smx, wg_pad, bg, onorm.reshape(1, HEAD_V))


def _conv_silu(u, w_ref, rowlen):
    n = u.shape[0]
    t = lax.broadcasted_iota(I32, (n, 1), 0) % rowlen
    acc = u * w_ref[CONV_W // 2:CONV_W // 2 + 1, :]
    for j in range(CONV_W):
        d = j - CONV_W // 2
        if d == 0:
            continue
        shifted = pltpu.roll(u, (-d) % n, 0)
        valid = (t + d >= 0) & (t + d < rowlen)
        acc = acc + jnp.where(valid, shifted, 0.0) * w_ref[j:j + 1, :]
    return _silu(acc)


def _l2norm(u):
    return u * lax.rsqrt(jnp.sum(u * u, axis=-1, keepdims=True) + EPS)


def _gdn_prep_all(qn, kn, vv, par, w2_ref, nc_ref, oc_ref, eg_ref, chains, eye, incl, strict):
    last_row = (CHUNK - 1, 0)
    rows = [pl.ds(r0, CHUNK) for _, r0, _ in chains]
    dirs = [d for _, _, d in chains]
    q = [qn[r, :] for r in rows]
    k = [kn[r, :] for r in rows]
    v = [vv[r, :] for r in rows]
    p = [par[r, :] for r in rows]
    gam = [pi[:, 4 + d:5 + d] for pi, d in zip(p, dirs)]
    beta = [pi[:, 2 + d:3 + d] for pi, d in zip(p, dirs)]
    gl = [g[last_row[d]:last_row[d] + 1, :] for g, d in zip(gam, dirs)]
    kb = [ki * bi for ki, bi in zip(k, beta)]
    kbf = [ki.astype(BF16) for ki in k]
    both = [_dot_nt(jnp.concatenate([kbi, qi], axis=0).astype(BF16), kf)
            for kbi, qi, kf in zip(kb, q, kbf)]
    decay = []
    for g, d in zip(gam, dirs):
        gam_c = jnp.broadcast_to(g, (CHUNK, CHUNK))
        gam_r = jnp.sum(jnp.where(eye, gam_c, 0.0), axis=0, keepdims=True)
        decay.append(jnp.exp(jnp.where(incl[d], gam_c - gam_r, -jnp.inf)))
    eg = [jnp.exp(g) for g in gam]
    rhs = [jnp.concatenate([kbi * egi, vi * bi], axis=1).astype(BF16)
           for kbi, egi, vi, bi in zip(kb, eg, v, beta)]
    kdt = [_transpose_chunk(ki * jnp.exp(gli - g)) for ki, gli, g in zip(k, gl, gam)]
    a = [jnp.where(strict[d], bo[:CHUNK] * de, 0.0) for bo, de, d in zip(both, decay, dirs)]
    qk = [bo[CHUNK:] * de for bo, de in zip(both, decay)]
    ab = [ai.astype(BF16) for ai in a]
    x = [jnp.where(eye, 1.0, 0.0) - ai for ai in a]
    pw = [_dot(abi, abi) for abi in ab]
    n_doublings = 5
    for it in range(n_doublings):
        pb = [pi.astype(BF16) for pi in pw]
        x = [xi + _dot(xi.astype(BF16), pbi) for xi, pbi in zip(x, pb)]
        if it + 1 < n_doublings:
            pw = [_dot(pbi, pbi) for pbi in pb]
    sol = [_dot(xi.astype(BF16), ri).astype(BF16) for xi, ri in zip(x, rhs)]
    lr = [_dot(jnp.concatenate([kt, qki], axis=0).astype(BF16), si)
          for kt, qki, si in zip(kdt, qk, sol)]
    for (slot, _, _), qi, egi, gli, lri in zip(chains, q, eg, gl, lr):
        w2_ref[slot] = jnp.concatenate(
            [lri[:HEAD_K, :HEAD_K], qi * egi - lri[HEAD_K:, :HEAD_K]], axis=0).astype(BF16)
        nc_ref[slot] = lri[:HEAD_K, HEAD_K:]
        oc_ref[slot] = lri[HEAD_K:, HEAD_K:]
        eg_ref[slot] = jnp.broadcast_to(jnp.exp(gli), (SUBLANES, HEAD_V))


def _gdn_step_pair(o_ref, w2_ref, nc_ref, oc_ref, eg_ref, s_refs, pair):
    s = [s_ref[...] for s_ref in s_refs]
    rm = [_dot(w2_ref[slot], si.astype(BF16)) for (slot, _), si in zip(pair, s)]
    for (slot, r0), s_ref, si, ri in zip(pair, s_refs, s, rm):
        o_ref[pl.ds(r0, CHUNK), :] += ri[HEAD_K:] + oc_ref[slot]
        s_ref[...] = si * eg_ref[slot][0:1, :] + nc_ref[slot] - ri[:HEAD_K]


def _gdn_kernel(alog_ref, dt_ref, qc, kc, vc, zc, smc, qx, kx, vx, zx, smx, wq_ref, wk_ref, wv_ref, on_ref,
                yc_ref, yx_ref, qnc, knc, vvc, parc, qnx, knx, vvx, parx, oc_scr, ox_scr, sf_ref, sb_ref,
                w2_ref, nc_ref, oc_ref, eg_ref):
    lc = qc.shape[1]
    lx = qx.shape[1]
    h = pl.program_id(1)

    row = lax.broadcasted_iota(I32, (N_SMALL, LANES), 0)
    lane = lax.broadcasted_iota(I32, (N_SMALL, LANES), 1)
    pick = ((row == h + N_HEADS * lane) & (lane < 4)).astype(BF16)
    lane1 = lax.broadcasted_iota(I32, (1, LANES), 1)
    alog = jnp.where(lane1 == 0, alog_ref[0, h], alog_ref[1, h])
    dtb = jnp.where(lane1 == 0, dt_ref[0, h], dt_ref[1, h])
    neg_a = -jnp.exp(alog)

    def prep(q_ref, k_ref, v_ref, sm_ref, qn, kn, vv, par, seq, rowlen):
        blk = rowlen if rowlen > CHUNK else min(256, seq)

        def body(r, carry):
            rows = pl.ds(pl.multiple_of(r * blk, blk), blk)
            qn[rows, :] = _l2norm(_conv_silu(q_ref[0, rows, :].astype(F32), wq_ref, rowlen)) * Q_SCALE
            kn[rows, :] = _l2norm(_conv_silu(k_ref[0, rows, :].astype(F32), wk_ref, rowlen))
            vv[rows, :] = _conv_silu(v_ref[0, rows, :].astype(F32), wv_ref, rowlen)
            picked = _dot_exact_rhs01(sm_ref[0, rows, :], pick)
            pv = jnp.where(lane1 < 2, neg_a * _softplus(picked + dtb), jax.nn.sigmoid(picked))
            cum_f = pltpu.roll(_chunk_cumsum(pv, False), 4, 1)
            cum_b = pltpu.roll(_chunk_cumsum(pv, True), 4, 1)
            par[rows, :] = jnp.where(lane1 == 4, cum_f, jnp.where(lane1 == 5, cum_b, pv))
            return carry

        lax.fori_loop(0, seq // blk, body, 0)

    prep(qc, kc, vc, smc, qnc, knc, vvc, parc, lc, lc)
    prep(qx, kx, vx, smx, qnx, knx, vvx, parx, lx, GRID_W)

    r, c = _chunk_masks()
    lower = c <= r
    upper = c >= r
    eye = c == r

    sf_ref[...] = jnp.zeros_like(sf_ref)
    sb_ref[...] = jnp.zeros_like(sb_ref)
    _zero_rows(oc_scr, lc)
    _zero_rows(ox_scr, lx)

    def run(qn, kn, vv, par, o_scr, seq):
        def prep_all(chains):
            _gdn_prep_all(qn, kn, vv, par, w2_ref, nc_ref, oc_ref, eg_ref, chains, eye,
                          (lower, upper), (c < r, c > r))

        def step_pair(fwd, bwd):
            _gdn_step_pair(o_scr, w2_ref, nc_ref, oc_ref, eg_ref, (sf_ref, sb_ref), (fwd, bwd))

        _bidir_blocks(seq // CHUNK, prep_all, step_pair)

    run(qnc, knc, vvc, parc, oc_scr, lc)
    run(qnx, knx, vvx, parx, ox_scr, lx)
    _head_finalize(oc_scr, zc, on_ref, yc_ref, lc)
    _head_finalize(ox_scr, zx, on_ref, yx_ref, lx)


def _gdn_mix(pc, smc, px, smx, conv_w, a_log, dt_bias, onorm):
    bsz, lc, _ = pc.shape
    lx = px.shape[1]
    assert lc % (K_CHUNKS * CHUNK) == 0 and lx % (K_CHUNKS * CHUNK) == 0 and lx % GRID_W == 0
    kblk = HK // HEAD_K
    vblk = 2 * HK // HEAD_V
    smem = pl.BlockSpec(memory_space=pltpu.SMEM)
    return pl.pallas_call(
        _gdn_kernel,
        grid=(bsz, N_HEADS),
        in_specs=[smem, smem] + _head_specs(lc) + _head_specs(lx) + [
            pl.BlockSpec((CONV_W, HEAD_K), lambda b, h: (0, h)),
            pl.BlockSpec((CONV_W, HEAD_K), lambda b, h: (0, kblk + h)),
            pl.BlockSpec((CONV_W, HEAD_V), lambda b, h: (0, vblk + h)),
            pl.BlockSpec((1, HEAD_V), lambda b, h: (0, 0)),
        ],
        out_specs=[
            pl.BlockSpec((1, lc, HEAD_V), lambda b, h: (b, 0, h)),
            pl.BlockSpec((1, lx, HEAD_V), lambda b, h: (b, 0, h)),
        ],
        out_shape=[
            jax.ShapeDtypeStruct((bsz, lc, HV), BF16),
            jax.ShapeDtypeStruct((bsz, lx, HV), BF16),
        ],
        scratch_shapes=[
            pltpu.VMEM((lc, HEAD_K), F32), pltpu.VMEM((lc, HEAD_K), F32),
            pltpu.VMEM((lc, HEAD_V), F32), pltpu.VMEM((lc, LANES), F32),
            pltpu.VMEM((lx, HEAD_K), F32), pltpu.VMEM((lx, HEAD_K), F32),
            pltpu.VMEM((lx, HEAD_V), F32), pltpu.VMEM((lx, LANES), F32),
            pltpu.VMEM((lc, HEAD_V), F32), pltpu.VMEM((lx, HEAD_V), F32),
            pltpu.VMEM((HEAD_K, HEAD_V), F32), pltpu.VMEM((HEAD_K, HEAD_V), F32),
            pltpu.VMEM((2 * K_CHUNKS, HEAD_K + CHUNK, HEAD_K), BF16),
            pltpu.VMEM((2 * K_CHUNKS, HEAD_K, HEAD_V), F32),
            pltpu.VMEM((2 * K_CHUNKS, CHUNK, HEAD_V), F32),
            pltpu.VMEM((2 * K_CHUNKS, SUBLANES, HEAD_V), F32),
        ],
        compiler_params=_cparams("parallel", "parallel"),
        name="gdn_bidir",
    )(a_log, dt_bias, pc, pc, pc, pc, smc, px, px, px, px, smx, conv_w, conv_w, conv_w,
      onorm.reshape(1, HEAD_V))


def _outproj_kernel(y_ref, w_ref, x_ref, g_ref, o_ref):
    o_ref[0] = x_ref[0] + g_ref[0] * _dot(y_ref[0], w_ref[...])


def _outproj(y, w, x, mod_l, mod_row, gate_blk):
    bsz, seq, d = x.shape
    kdim = y.shape[2]
    tm = min(512, seq)
    tn = min(1024, d)
    nblk = d // tn
    return pl.pallas_call(
        _outproj_kernel,
        grid=(bsz, seq // tm, nblk),
        in_specs=[
            pl.BlockSpec((1, tm, kdim), lambda b, i, j: (b, i, 0)),
            pl.BlockSpec((kdim, tn), lambda b, i, j: (0, j)),
            pl.BlockSpec((1, tm, tn), lambda b, i, j: (b, i, j)),
            pl.BlockSpec((1, 1, tn), lambda b, i, j: (mod_row(b), 0, gate_blk * nblk + j)),
        ],
        out_specs=pl.BlockSpec((1, tm, tn), lambda b, i, j: (b, i, j)),
        out_shape=jax.ShapeDtypeStruct(x.shape, F32),
        compiler_params=_cparams("parallel", "parallel", "parallel"),
        name="outproj_residual",
    )(y, w, x, mod_l)


def _router_kernel(x_ref, g_ref, sh_ref, sc_ref, wr_ref, a_ref, hi_ref, lo_ref):
    tm = hi_ref.shape[0]
    slab = _row_slab(tm)
    gv = g_ref[...]
    scale1 = 1.0 + sc_ref[0]
    shift = sh_ref[0]

    def body(r, carry):
        rows = pl.ds(pl.multiple_of(r * slab, slab), slab)
        h = _norm_mod_rows(x_ref[0, rows, :], gv, scale1, shift)
        hi = h.astype(BF16)
        hi_ref[rows, :] = hi
        lo_ref[rows, :] = (h - hi.astype(F32)).astype(BF16)
        return carry

    lax.fori_loop(0, tm // slab, body, 0)
    w = wr_ref[...]
    w_hi = w.astype(BF16)
    w_lo = (w - w_hi.astype(F32)).astype(BF16)
    hi = hi_ref[...]
    logits = _dot_nt(w_hi, hi) + _dot_nt(w_lo, hi) + _dot_nt(w_hi, lo_ref[...])
    m = jnp.max(logits, axis=0, keepdims=True)
    e = jnp.exp(logits - m)
    a_ref[0] = e / jnp.sum(e, axis=0, keepdims=True)


def _router(x, norm_g, mod_l, mod_row, w_router_t):
    bsz, seq, d = x.shape
    tm = min(512, seq)
    return pl.pallas_call(
        _router_kernel,
        grid=(bsz, seq // tm),
        in_specs=[
            pl.BlockSpec((1, tm, d), lambda b, i: (b, i, 0)),
            pl.BlockSpec((1, d), lambda b, i: (0, 0)),
            pl.BlockSpec((1, 1, d), lambda b, i: (mod_row(b), 0, 3)),
            pl.BlockSpec((1, 1, d), lambda b, i: (mod_row(b), 0, 4)),
            pl.BlockSpec((N_EXPERTS, d), lambda b, i: (0, 0)),
        ],
        out_specs=pl.BlockSpec((1, N_EXPERTS, tm), lambda b, i: (b, 0, i)),
        out_shape=jax.ShapeDtypeStruct((bsz, N_EXPERTS, seq), F32),
        scratch_shapes=[pltpu.VMEM((tm, d), BF16), pltpu.VMEM((tm, d), BF16)],
        compiler_params=_cparams("parallel", "parallel"),
        name="router_affinity",
    )(x, norm_g.reshape(1, d), mod_l, mod_l, w_router_t)


def _lane_cumsum(src_ref, dst_ref, seq):
    r = lax.broadcasted_iota(I32, (LANES, LANES), 0)
    c = lax.broadcasted_iota(I32, (LANES, LANES), 1)
    upper = (r <= c).astype(BF16)
    carry = jnp.zeros((N_EXPERTS, 1), F32)
    for jb in range(seq // LANES):
        sl = slice(jb * LANES, (jb + 1) * LANES)
        cs = _dot(src_ref[:, sl].astype(BF16), upper) + carry
        dst_ref[:, sl] = cs
        carry = cs[:, LANES - 1:LANES]


def _topk_kernel(a_ref, idx_ref, tok_ref, m_scr, cum_scr, pack_scr, *, cap):
    a = a_ref[0]
    seq = a.shape[1]
    bits = lax.bitcast_convert_type(a, I32)

    def search(i, prefix):
        cand = prefix | jnp.left_shift(jnp.int32(1), 30 - i)
        cnt = jnp.sum((bits >= cand).astype(F32), axis=1, keepdims=True)
        return jnp.where(cnt >= cap, cand, prefix)

    thr = lax.fori_loop(0, 31, search, jnp.zeros((N_EXPERTS, 1), I32))
    gt = bits > thr
    eq = bits == thr
    need = cap - jnp.sum(gt.astype(F32), axis=1, keepdims=True)
    m_scr[...] = eq.astype(F32)
    _lane_cumsum(m_scr, cum_scr, seq)
    sel = gt | (eq & (cum_scr[...] - 1.0 < need))
    m_scr[...] = sel.astype(F32)
    _lane_cumsum(m_scr, cum_scr, seq)

    pack_scr[...] = jnp.zeros_like(pack_scr)
    pack_scr[0:N_EXPERTS, :] = jnp.where(sel, cum_scr[...] - 1.0, -1.0)
    pack_scr[N_EXPERTS:2 * N_EXPERTS, :] = jnp.where(sel, a, 0.0)
    tok_ref[0] = pack_scr[...].T

    cap_pad = idx_ref.shape[2]
    ones = jnp.ones((SUBLANES, LANES), BF16)
    for e in range(N_EXPERTS):
        for pb in range(cap_pad // LANES):
            slot = (lax.broadcasted_iota(I32, (LANES, 1), 0) + pb * LANES).astype(F32)

            def count_block(jb, acc, e=e, slot=slot):
                row = cum_scr[e:e + 1, pl.ds(pl.multiple_of(jb * LANES, LANES), LANES)]
                return acc + (row <= slot).astype(F32)

            acc = lax.fori_loop(0, seq // LANES, count_block, jnp.zeros((LANES, LANES), F32))
            counts = _dot_nt(ones, acc.astype(BF16))
            idx_ref[0, e:e + 1, pb * LANES:(pb + 1) * LANES] = (
                jnp.minimum(counts[0:1, :], seq - 1.0).astype(I32))


def _topk(aff_t, cap, cap_pad):
    bsz, _, seq = aff_t.shape
    return pl.pallas_call(
        functools.partial(_topk_kernel, cap=cap),
        grid=(bsz,),
        in_specs=[pl.BlockSpec((1, N_EXPERTS, seq), lambda b: (b, 0, 0))],
        out_specs=[
            pl.BlockSpec((1, N_EXPERTS, cap_pad), lambda b: (b, 0, 0)),
            pl.BlockSpec((1, seq, LANES), lambda b: (b, 0, 0)),
        ],
        out_shape=[
            jax.ShapeDtypeStruct((bsz, N_EXPERTS, cap_pad), I32),
            jax.ShapeDtypeStruct((bsz, seq, LANES), F32),
        ],
        scratch_shapes=[
            pltpu.VMEM((N_EXPERTS, seq), F32), pltpu.VMEM((N_EXPERTS, seq), F32),
            pltpu.VMEM((LANES, seq), F32),
        ],
        compiler_params=_cparams("parallel"),
        name="expert_choice_select",
    )(aff_t)


def _expert_kernel(idx_ref, x_hbm, g_ref, sh_ref, sc_ref, wg_ref, wu_ref, o_ref, xs_ref, sem, *, cap, cap_pad):
    nb = pl.num_programs(1)
    e = pl.program_id(0)
    b = pl.program_id(1)
    step = e * nb + b
    cur = step % 2

    def row_copy(buf, bb, p, t):
        return pltpu.make_async_copy(x_hbm.at[bb, pl.ds(t, 1), :], xs_ref.at[buf, pl.ds(p, 1), :], sem.at[buf])

    def gather(buf, ee, bb):
        base = (bb * N_EXPERTS + ee) * cap_pad

        def issue(p, carry):
            row_copy(buf, bb, p, idx_ref[base + p]).start()
            return carry

        lax.fori_loop(0, cap, issue, 0, unroll=8)

    @pl.when(step == 0)
    def _():
        if cap < cap_pad:
            xs_ref[:, cap:cap_pad, :] = jnp.zeros((2, cap_pad - cap, xs_ref.shape[2]), F32)
        gather(0, e, b)

    @pl.when(step + 1 < pl.num_programs(0) * nb)
    def _():
        gather(1 - cur, (step + 1) // nb, (step + 1) % nb)

    def drain(p, carry):
        row_copy(cur, b, p, 0).wait()
        return carry

    lax.fori_loop(0, cap, drain, 0, unroll=8)

    slab = _row_slab(cap_pad)
    gv = g_ref[...]
    scale1 = 1.0 + sc_ref[0]
    shift = sh_ref[0]

    def body(r, carry):
        rows = pl.ds(pl.multiple_of(r * slab, slab), slab)
        hs = _norm_mod_rows(xs_ref[cur, rows, :], gv, scale1, shift).astype(BF16)
        hid = _silu(_dot(hs, wg_ref[0])) * _dot(hs, wu_ref[0])
        o_ref[0, 0, rows, :] = hid.astype(o_ref.dtype)
        return carry

    lax.fori_loop(0, cap_pad // slab, body, 0)


def _experts(idx, cap, x, norm_g, mod_l, mod_row, w_gate, w_up):
    bsz, seq, d = x.shape
    cap_pad = idx.shape[2]
    grid_spec = pltpu.PrefetchScalarGridSpec(
        num_scalar_prefetch=1,
        grid=(N_EXPERTS, bsz),
        in_specs=[
            pl.BlockSpec(memory_space=pl.ANY),
            pl.BlockSpec((1, d), lambda e, b, idx_ref: (0, 0)),
            pl.BlockSpec((1, 1, d), lambda e, b, idx_ref: (mod_row(b), 0, 3)),
            pl.BlockSpec((1, 1, d), lambda e, b, idx_ref: (mod_row(b), 0, 4)),
            pl.BlockSpec((1, d, EXPERT_FF), lambda e, b, idx_ref: (e, 0, 0)),
            pl.BlockSpec((1, d, EXPERT_FF), lambda e, b, idx_ref: (e, 0, 0)),
        ],
        out_specs=pl.BlockSpec((1, 1, cap_pad, EXPERT_FF), lambda e, b, idx_ref: (b, e, 0, 0)),
        scratch_shapes=[pltpu.VMEM((2, cap_pad, d), F32), pltpu.SemaphoreType.DMA((2,))],
    )
    return pl.pallas_call(
        functools.partial(_expert_kernel, cap=cap, cap_pad=cap_pad),
        grid_spec=grid_spec,
        out_shape=jax.ShapeDtypeStruct((bsz, N_EXPERTS, cap_pad, EXPERT_FF), BF16),
        compiler_params=_cparams("arbitrary", "arbitrary"),
        name="expert_gather_swiglu",
    )(idx.reshape(-1), x, norm_g.reshape(1, d), mod_l, mod_l, w_gate, w_up)


def _down_kernel(tok_ref, hid_ref, wd_ref, x_ref, g_ref, o_ref, hcat_ref):
    tm = hcat_ref.shape[0]
    cap_pad = hid_ref.shape[2]

    @pl.when(pl.program_id(2) == 0)
    def _():
        tok = tok_ref[0]
        slot = lax.broadcasted_iota(I32, (tm, cap_pad), 1).astype(F32)
        for e in range(N_EXPERTS):
            onehot = (tok[:, e:e + 1] == slot).astype(BF16)
            rows = _dot(onehot, hid_ref[0, e])
            hcat_ref[:, e * EXPERT_FF:(e + 1) * EXPERT_FF] = (
                rows * tok[:, N_EXPERTS + e:N_EXPERTS + e + 1]).astype(BF16)

    o_ref[0] = x_ref[0] + g_ref[0] * _dot(hcat_ref[...], wd_ref[...])


def _combine_down(tok, hid, w_down_cat, x, mod_l, mod_row):
    bsz, seq, d = x.shape
    cap_pad = hid.shape[2]
    kdim = N_EXPERTS * EXPERT_FF
    tm = min(512, seq)
    tn = min(1024, d)
    nblk = d // tn
    return pl.pallas_call(
        _down_kernel,
        grid=(bsz, seq // tm, nblk),
        in_specs=[
            pl.BlockSpec((1, tm, LANES), lambda b, i, j: (b, i, 0)),
            pl.BlockSpec((1, N_EXPERTS, cap_pad, EXPERT_FF), lambda b, i, j: (b, 0, 0, 0)),
            pl.BlockSpec((kdim, tn), lambda b, i, j: (0, j)),
            pl.BlockSpec((1, tm, tn), lambda b, i, j: (b, i, j)),
            pl.BlockSpec((1, 1, tn), lambda b, i, j: (mod_row(b), 0, 5 * nblk + j)),
        ],
        out_specs=pl.BlockSpec((1, tm, tn), lambda b, i, j: (b, i, j)),
        out_shape=jax.ShapeDtypeStruct(x.shape, F32),
        scratch_shapes=[pltpu.VMEM((tm, kdim), BF16)],
        compiler_params=_cparams("parallel", "parallel", "arbitrary"),
        name="combine_down_residual",
    )(tok, hid, w_down_cat, x, mod_l)


def _ec_ffn_residual(x, norm_g, mod_l, mod_row, w_router_t, w_gate, w_up, w_down_cat):
    seq = x.shape[1]
    cap = CAPACITY_FACTOR * seq // N_EXPERTS
    cap_pad = -(-cap // LANES) * LANES
    aff_t = _router(x, norm_g, mod_l, mod_row, w_router_t)
    idx, tok = _topk(aff_t, cap, cap_pad)
    hid = _experts(idx, cap, x, norm_g, mod_l, mod_row, w_gate, w_up)
    return _combine_down(tok, hid, w_down_cat, x, mod_l, mod_row)


def _final_kernel(x_ref, g_ref, o_ref):
    tm = x_ref.shape[1]
    slab = _row_slab(tm)
    gv = g_ref[...]

    def body(r, carry):
        rows = pl.ds(pl.multiple_of(r * slab, slab), slab)
        x = x_ref[0, rows, :]
        ms = jnp.mean(x * x, axis=-1, keepdims=True)
        o_ref[0, rows, :] = x * lax.rsqrt(ms + EPS) * gv
        return carry

    lax.fori_loop(0, tm // slab, body, 0)


def _final_norm(x, g):
    bsz, seq, d = x.shape
    tm = min(512, seq)
    return pl.pallas_call(
        _final_kernel,
        grid=(bsz, seq // tm),
        in_specs=[
            pl.BlockSpec((1, tm, d), lambda b, i: (b, i, 0)),
            pl.BlockSpec((1, d), lambda b, i: (0, 0)),
        ],
        out_specs=pl.BlockSpec((1, tm, d), lambda b, i: (b, i, 0)),
        out_shape=jax.ShapeDtypeStruct(x.shape, F32),
        compiler_params=_cparams("parallel", "parallel"),
        name="final_rmsnorm",
    )(x, g.reshape(1, d))


def _split_inproj_weight(w_in):
    w_main = w_in[:, :N_MAIN].astype(BF16)
    w_small = w_in[:, N_MAIN:]
    w_small = jnp.pad(w_small, ((0, 0), (0, N_SMALL - w_small.shape[1]))).astype(BF16)
    return w_main, w_small


def kernel(x, c, ctx, c_ctx, ada_w, ada_b, norm1_g, norm2_g, gla_w_in, gla_wg2, gla_bg2, gla_onorm, gla_w_out, gdn_w_in, gdn_conv, gdn_A_log, gdn_dt_bias, gdn_onorm, gdn_w_out, router_w, exp_w_gate, exp_w_up, exp_w_down, final_g):
    bsz, _, d = x.shape
    depth = ada_w.shape[0]
    n_rows = -(-(bsz + 1) // SUBLANES) * SUBLANES
    c_rows = jnp.zeros((n_rows, d), F32).at[:bsz].set(c).at[bsz].set(c_ctx)
    mod = _modulation(c_rows, ada_w, ada_b)

    def lat_row(b):
        return b

    def ctx_row(b):
        return bsz

    cx = ctx
    for i in range(depth):
        last = i == depth - 1
        j = i // 2
        mod_l = mod[i].reshape(n_rows, 1, 6 * d)
        if i % 2 == 0:
            w_main, w_small = _split_inproj_weight(gla_w_in[j])
            w_out = gla_w_out[j].astype(BF16)
        else:
            w_main, w_small = _split_inproj_weight(gdn_w_in[j])
            w_out = gdn_w_out[j].astype(BF16)
        pc, smc = _inproj(cx, norm1_g[i], mod_l, ctx_row, w_main, w_small)
        px, smx = _inproj(x, norm1_g[i], mod_l, lat_row, w_main, w_small)
        if i % 2 == 0:
            wg_pad = jnp.zeros((2, N_SMALL, HK), F32)
            wg_pad = wg_pad.at[0, :GLA_LOWRANK].set(gla_wg2[j, 0])
            wg_pad = wg_pad.at[1, GLA_LOWRANK:2 * GLA_LOWRANK].set(gla_wg2[j, 1]).astype(BF16)
            yc, yx = _gla_mix(pc, smc, px, smx, wg_pad, gla_bg2[j].reshape(2, 1, HK), gla_onorm[j])
        else:
            yc, yx = _gdn_mix(pc, smc, px, smx, gdn_conv[j], gdn_A_log[j], gdn_dt_bias[j], gdn_onorm[j])
        w_router_t = router_w[i].T
        w_gate = exp_w_gate[i].astype(BF16)
        w_up = exp_w_up[i].astype(BF16)
        w_down_cat = exp_w_down[i].reshape(N_EXPERTS * EXPERT_FF, d).astype(BF16)
        x = _outproj(yx, w_out, x, mod_l, lat_row, 2)
        x = _ec_ffn_residual(x, norm2_g[i], mod_l, lat_row, w_router_t, w_gate, w_up, w_down_cat)
        if not last:
            cx = _outproj(yc, w_out, cx, mod_l, ctx_row, 2)
            cx = _ec_ffn_residual(cx, norm2_g[i], mod_l, ctx_row, w_router_t, w_gate, w_up, w_down_cat)
    return _final_norm(x, final_g)
```

```python
import functools

import jax
import jax.numpy as jnp
from jax import lax
from jax.experimental import pallas as pl
from jax.experimental.pallas import tpu as pltpu

F32 = jnp.float32
BF16 = jnp.bfloat16
I32 = jnp.int32

N_HEADS = 16
HEAD_K = 128
HEAD_V = 256
HK = N_HEADS * HEAD_K
HV = N_HEADS * HEAD_V
CHUNK = 64
GRID_W = 64
GLA_LOWRANK = 16
GLA_GATE_NORM = 16.0
CONV_W = 5
N_EXPERTS = 16
EXPERT_FF = 256
CAPACITY_FACTOR = 2
EPS = 1e-6
N_MAIN = 2 * HK + 2 * HV
N_SMALL = 128
LANES = 128
SUBLANES = 8
VMEM_LIMIT_BYTES = 56 * 1024 * 1024
Q_SCALE = HEAD_K ** -0.5
K_CHUNKS = 8


def _cparams(*sem):
    return pltpu.CompilerParams(dimension_semantics=sem, vmem_limit_bytes=VMEM_LIMIT_BYTES)


def _dot(a, b):
    return jnp.dot(a, b, preferred_element_type=F32)


def _dot_nt(a, b):
    return lax.dot_general(a, b, (((1,), (1,)), ((), ())), preferred_element_type=F32)


def _silu(x):
    return x * jax.nn.sigmoid(x)


def _log_sigmoid(x):
    return jnp.minimum(x, 0.0) - jnp.log1p(jnp.exp(-jnp.abs(x)))


def _softplus(x):
    return jnp.maximum(x, 0.0) + jnp.log1p(jnp.exp(-jnp.abs(x)))


def _split3(x):
    hi = x.astype(BF16)
    r = x - hi.astype(F32)
    mid = r.astype(BF16)
    lo = (r - mid.astype(F32)).astype(BF16)
    return hi, mid, lo


def _dot_exact_rhs01(x, m01):
    hi, mid, lo = _split3(x)
    return _dot(hi, m01) + _dot(mid, m01) + _dot(lo, m01)


def _chunk_cumsum(x, reverse):
    n = x.shape[0]
    t = lax.broadcasted_iota(I32, (n, 1), 0) % CHUNK
    s = 1
    while s < CHUNK:
        if reverse:
            shifted = pltpu.roll(x, n - s, 0)
            valid = t < CHUNK - s
        else:
            shifted = pltpu.roll(x, s, 0)
            valid = t >= s
        x = x + jnp.where(valid, shifted, 0.0)
        s *= 2
    return x


def _transpose_chunk(x):
    pad = jnp.zeros((LANES - CHUNK, x.shape[1]), F32)
    return jnp.concatenate([x, pad], axis=0).T[:, :CHUNK]


def _chunk_masks():
    r = lax.broadcasted_iota(I32, (CHUNK, CHUNK), 0)
    c = lax.broadcasted_iota(I32, (CHUNK, CHUNK), 1)
    return r, c


def _mod_kernel(c_ref, w_ref, b_ref, o_ref, acc_ref):
    k = pl.program_id(2)

    @pl.when(k == 0)
    def _():
        acc_ref[...] = jnp.zeros_like(acc_ref)

    acc_ref[...] += _dot(_silu(c_ref[...]).astype(BF16), w_ref[0].astype(BF16))

    @pl.when(k == pl.num_programs(2) - 1)
    def _():
        o_ref[0] = acc_ref[...] + b_ref[0]


def _modulation(c_rows, ada_w, ada_b):
    depth, d, n = ada_w.shape
    rows = c_rows.shape[0]
    tk = min(1024, d)
    tn = min(2048, n)
    return pl.pallas_call(
        _mod_kernel,
        grid=(depth, n // tn, d // tk),
        in_specs=[
            pl.BlockSpec((rows, tk), lambda l, j, k: (0, k)),
            pl.BlockSpec((1, tk, tn), lambda l, j, k: (l, k, j)),
            pl.BlockSpec((1, 1, tn), lambda l, j, k: (l, 0, j)),
        ],
        out_specs=pl.BlockSpec((1, rows, tn), lambda l, j, k: (l, 0, j)),
        out_shape=jax.ShapeDtypeStruct((depth, rows, n), F32),
        scratch_shapes=[pltpu.VMEM((rows, tn), F32)],
        compiler_params=_cparams("parallel", "parallel", "arbitrary"),
        name="adaln_modulation",
    )(c_rows, ada_w, ada_b.reshape(depth, 1, n))


def _norm_mod_rows(x, gv, scale1, shift):
    ms = jnp.mean(x * x, axis=-1, keepdims=True)
    return (x * lax.rsqrt(ms + EPS) * gv) * scale1 + shift


def _row_slab(tm):
    return 64 if tm % 64 == 0 else tm


def _inproj_kernel(x_ref, g_ref, sh_ref, sc_ref, w_ref, ws_ref, o_ref, os_ref, h_ref):
    tm = h_ref.shape[0]
    slab = _row_slab(tm)

    @pl.when(pl.program_id(2) == 0)
    def _():
        gv = g_ref[...]
        scale1 = 1.0 + sc_ref[0]
        shift = sh_ref[0]

        def body(r, carry):
            rows = pl.ds(pl.multiple_of(r * slab, slab), slab)
            h_ref[rows, :] = _norm_mod_rows(x_ref[0, rows, :], gv, scale1, shift).astype(BF16)
            return carry

        lax.fori_loop(0, tm // slab, body, 0)
        os_ref[0] = _dot(h_ref[...], ws_ref[...])

    o_ref[0] = _dot(h_ref[...], w_ref[...]).astype(o_ref.dtype)


def _inproj(x, norm_g, mod_l, mod_row, w_main, w_small):
    bsz, seq, d = x.shape
    n = w_main.shape[1]
    tm = min(512, seq)
    tn = min(1024, n)
    return pl.pallas_call(
        _inproj_kernel,
        grid=(bsz, seq // tm, n // tn),
        in_specs=[
            pl.BlockSpec((1, tm, d), lambda b, i, j: (b, i, 0)),
            pl.BlockSpec((1, d), lambda b, i, j: (0, 0)),
            pl.BlockSpec((1, 1, d), lambda b, i, j: (mod_row(b), 0, 0)),
            pl.BlockSpec((1, 1, d), lambda b, i, j: (mod_row(b), 0, 1)),
            pl.BlockSpec((d, tn), lambda b, i, j: (0, j)),
            pl.BlockSpec((d, N_SMALL), lambda b, i, j: (0, 0)),
        ],
        out_specs=[
            pl.BlockSpec((1, tm, tn), lambda b, i, j: (b, i, j)),
            pl.BlockSpec((1, tm, N_SMALL), lambda b, i, j: (b, i, 0)),
        ],
        out_shape=[
            jax.ShapeDtypeStruct((bsz, seq, n), BF16),
            jax.ShapeDtypeStruct((bsz, seq, N_SMALL), F32),
        ],
        scratch_shapes=[pltpu.VMEM((tm, d), BF16)],
        compiler_params=_cparams("parallel", "parallel", "arbitrary"),
        name="norm_inproj",
    )(x, norm_g.reshape(1, d), mod_l, mod_l, w_main, w_small)


def _bidir_blocks(n_chunks, prep_all, step_pair):
    kc = min(K_CHUNKS, n_chunks)
    assert n_chunks % kc == 0
    nblk = n_chunks // kc

    def body(i, carry):
        fbase = pl.multiple_of(i * (kc * CHUNK), kc * CHUNK)
        bbase = pl.multiple_of((nblk - 1 - i) * (kc * CHUNK), kc * CHUNK)
        prep_all([(c, fbase + c * CHUNK, 0) for c in range(kc)]
                 + [(kc + c, bbase + c * CHUNK, 1) for c in range(kc)])
        for c in range(kc):
            step_pair((c, fbase + c * CHUNK), (2 * kc - 1 - c, bbase + (kc - 1 - c) * CHUNK))
        return carry

    lax.fori_loop(0, nblk, body, 0)


def _zero_rows(ref, seq):
    blk = min(256, seq)

    def body(r, carry):
        ref[pl.ds(pl.multiple_of(r * blk, blk), blk), :] = jnp.zeros((blk, ref.shape[1]), ref.dtype)
        return carry

    lax.fori_loop(0, seq // blk, body, 0)


def _gla_prep_all(q_ref, k_ref, v_ref, b_refs, o_ref, qe_ref, st_ref, nc_ref, chains, masks):
    last_row = (CHUNK - 1, 0)
    rows = [pl.ds(r0, CHUNK) for _, r0, _ in chains]
    dirs = [d for _, _, d in chains]
    q = [q_ref[0, r, :].astype(F32) * Q_SCALE for r in rows]
    k = [k_ref[0, r, :].astype(F32) for r in rows]
    vb = [v_ref[0, r, :] for r in rows]
    b = [b_refs[d][r, :] for r, d in zip(rows, dirs)]
    btot = [bi[last_row[d]:last_row[d] + 1, :] for bi, d in zip(b, dirs)]
    ref = [bi[CHUNK // 2:CHUNK // 2 + 1, :] for bi in b]
    qt = [(qi * jnp.exp(bi - ri)).astype(BF16) for qi, bi, ri in zip(q, b, ref)]
    kt = [(ki * jnp.exp(ri - bi)).astype(BF16) for ki, bi, ri in zip(k, b, ref)]
    scores = [_dot_nt(qi, ki) for qi, ki in zip(qt, kt)]
    stacked = [jnp.concatenate([ki * jnp.exp(bt - bi), jnp.broadcast_to(jnp.exp(bt), (LANES - CHUNK, HEAD_K))],
                               axis=0) for ki, bi, bt in zip(k, b, btot)]
    st = [s.T for s in stacked]
    att = [jnp.where(masks[d], sc, 0.0).astype(BF16) for sc, d in zip(scores, dirs)]
    intra = [_dot(ai, vi) for ai, vi in zip(att, vb)]
    inc = [_dot(si[:, :CHUNK].astype(BF16), vi) for si, vi in zip(st, vb)]
    for (slot, _, _), r, qi, bi, si, ni, oi in zip(chains, rows, q, b, st, inc, intra):
        qe_ref[slot] = (qi * jnp.exp(bi)).astype(BF16)
        st_ref[slot] = si
        nc_ref[slot] = ni
        o_ref[r, :] += oi


def _gla_step_pair(o_ref, qe_ref, st_ref, nc_ref, s_refs, pair):
    s = [s_ref[...] for s_ref in s_refs]
    inter = [_dot(qe_ref[slot], si.astype(BF16)) for (slot, _), si in zip(pair, s)]
    for (slot, r0), s_ref, si, oi in zip(pair, s_refs, s, inter):
        o_ref[pl.ds(r0, CHUNK), :] += oi
        s_ref[...] = si * st_ref[slot][:, CHUNK:CHUNK + 1] + nc_ref[slot]


def _head_finalize(o_scr, z_ref, on_ref, y_ref, seq):
    blk = min(256, seq)
    onv = on_ref[...]

    def body(r, carry):
        rows = pl.ds(pl.multiple_of(r * blk, blk), blk)
        o = o_scr[rows, :]
        ms = jnp.mean(o * o, axis=-1, keepdims=True)
        y = o * lax.rsqrt(ms + EPS) * onv
        z = z_ref[0, rows, :].astype(F32)
        y_ref[0, rows, :] = (y * _silu(z)).astype(y_ref.dtype)
        return carry

    lax.fori_loop(0, seq // blk, body, 0)


def _gla_kernel(qc, kc, vc, zc, smc, qx, kx, vx, zx, smx, wg_ref, bg_ref, on_ref,
                yc_ref, yx_ref, bfc, bbc, bfx, bbx, oc_scr, ox_scr, sf_ref, sb_ref, qe_ref, st_ref, nc_ref):
    lc = qc.shape[1]
    lx = qx.shape[1]

    def gates(sm_ref, bf_ref, bb_ref, seq):
        blk = min(256, seq)

        def body(r, carry):
            rows = pl.ds(pl.multiple_of(r * blk, blk), blk)
            s = sm_ref[0, rows, :].astype(BF16)
            gf = _log_sigmoid(_dot(s, wg_ref[0]) + bg_ref[0]) * (1.0 / GLA_GATE_NORM)
            gb = _log_sigmoid(_dot(s, wg_ref[1]) + bg_ref[1]) * (1.0 / GLA_GATE_NORM)
            bf_ref[rows, :] = _chunk_cumsum(gf, False)
            bb_ref[rows, :] = _chunk_cumsum(gb, True)
            return carry

        lax.fori_loop(0, seq // blk, body, 0)

    gates(smc, bfc, bbc, lc)
    gates(smx, bfx, bbx, lx)

    r, c = _chunk_masks()
    lower = c <= r
    upper = c >= r

    sf_ref[...] = jnp.zeros_like(sf_ref)
    sb_ref[...] = jnp.zeros_like(sb_ref)
    _zero_rows(oc_scr, lc)
    _zero_rows(ox_scr, lx)

    def run(q_ref, k_ref, v_ref, bf_ref, bb_ref, o_scr, seq):
        def prep_all(chains):
            _gla_prep_all(q_ref, k_ref, v_ref, (bf_ref, bb_ref), o_scr, qe_ref, st_ref, nc_ref, chains,
                          (lower, upper))

        def step_pair(fwd, bwd):
            _gla_step_pair(o_scr, qe_ref, st_ref, nc_ref, (sf_ref, sb_ref), (fwd, bwd))

        _bidir_blocks(seq // CHUNK, prep_all, step_pair)

    run(qc, kc, vc, bfc, bbc, oc_scr, lc)
    run(qx, kx, vx, bfx, bbx, ox_scr, lx)
    _head_finalize(oc_scr, zc, on_ref, yc_ref, lc)
    _head_finalize(ox_scr, zx, on_ref, yx_ref, lx)


def _head_specs(seq):
    kblk = HK // HEAD_K
    vblk = 2 * HK // HEAD_V
    zblk = (2 * HK + HV) // HEAD_V
    return [
        pl.BlockSpec((1, seq, HEAD_K), lambda b, h: (b, 0, h)),
        pl.BlockSpec((1, seq, HEAD_K), lambda b, h: (b, 0, kblk + h)),
        pl.BlockSpec((1, seq, HEAD_V), lambda b, h: (b, 0, vblk + h)),
        pl.BlockSpec((1, seq, HEAD_V), lambda b, h: (b, 0, zblk + h)),
        pl.BlockSpec((1, seq, N_SMALL), lambda b, h: (b, 0, 0)),
    ]


def _gla_mix(pc, smc, px, smx, wg_pad, bg, onorm):
    bsz, lc, _ = pc.shape
    lx = px.shape[1]
    assert lc % CHUNK == 0 and lx % CHUNK == 0
    return pl.pallas_call(
        _gla_kernel,
        grid=(bsz, N_HEADS),
        in_specs=_head_specs(lc) + _head_specs(lx) + [
            pl.BlockSpec((2, N_SMALL, HEAD_K), lambda b, h: (0, 0, h)),
            pl.BlockSpec((2, 1, HEAD_K), lambda b, h: (0, 0, h)),
            pl.BlockSpec((1, HEAD_V), lambda b, h: (0, 0)),
        ],
        out_specs=[
            pl.BlockSpec((1, lc, HEAD_V), lambda b, h: (b, 0, h)),
            pl.BlockSpec((1, lx, HEAD_V), lambda b, h: (b, 0, h)),
        ],
        out_shape=[
            jax.ShapeDtypeStruct((bsz, lc, HV), BF16),
            jax.ShapeDtypeStruct((bsz, lx, HV), BF16),
        ],
        scratch_shapes=[
            pltpu.VMEM((lc, HEAD_K), F32), pltpu.VMEM((lc, HEAD_K), F32),
            pltpu.VMEM((lx, HEAD_K), F32), pltpu.VMEM((lx, HEAD_K), F32),
            pltpu.VMEM((lc, HEAD_V), F32), pltpu.VMEM((lx, HEAD_V), F32),
            pltpu.VMEM((HEAD_K, HEAD_V), F32), pltpu.VMEM((HEAD_K, HEAD_V), F32),
            pltpu.VMEM((2 * K_CHUNKS, CHUNK, HEAD_K), BF16),
            pltpu.VMEM((2 * K_CHUNKS, HEAD_K, LANES), F32),
            pltpu.VMEM((2 * K_CHUNKS, HEAD_K, HEAD_V), F32),
        ],
        compiler_params=_cparams("parallel", "parallel"),
        name="gla_bidir",
    )(pc, pc, pc, pc, smc, px, px, px, px, smx, wg_pad, bg, onorm.reshape(1, HEAD_V))


def _conv_silu(u, w_ref, rowlen):
    n = u.shape[0]
    t = lax.broadcasted_iota(I32, (n, 1), 0) % rowlen
    acc = u * w_ref[CONV_W // 2:CONV_W // 2 + 1, :]
    for j in range(CONV_W):
        d = j - CONV_W // 2
        if d == 0:
            continue
        shifted = pltpu.roll(u, (-d) % n, 0)
        valid = (t + d >= 0) & (t + d < rowlen)
        acc = acc + jnp.where(valid, shifted, 0.0) * w_ref[j:j + 1, :]
    return _silu(acc)


def _l2norm(u):
    return u * lax.rsqrt(jnp.sum(u * u, axis=-1, keepdims=True) + EPS)


def _gdn_prep_all(qn, kn, vv, par, w2_ref, nc_ref, oc_ref, eg_ref, chains, eye, incl, strict):
    last_row = (CHUNK - 1, 0)
    rows = [pl.ds(r0, CHUNK) for _, r0, _ in chains]
    dirs = [d for _, _, d in chains]
    q = [qn[r, :] for r in rows]
    k = [kn[r, :] for r in rows]
    v = [vv[r, :] for r in rows]
    p = [par[r, :] for r in rows]
    gam = [pi[:, 4 + d:5 + d] for pi, d in zip(p, dirs)]
    beta = [pi[:, 2 + d:3 + d] for pi, d in zip(p, dirs)]
    gl = [g[last_row[d]:last_row[d] + 1, :] for g, d in zip(gam, dirs)]
    kb = [ki * bi for ki, bi in zip(k, beta)]
    kbf = [ki.astype(BF16) for ki in k]
    both = [_dot_nt(jnp.concatenate([kbi, qi], axis=0).astype(BF16), kf)
            for kbi, qi, kf in zip(kb, q, kbf)]
    decay = []
    for g, d in zip(gam, dirs):
        gam_c = jnp.broadcast_to(g, (CHUNK, CHUNK))
        gam_r = jnp.sum(jnp.where(eye, gam_c, 0.0), axis=0, keepdims=True)
        decay.append(jnp.exp(jnp.where(incl[d], gam_c - gam_r, -jnp.inf)))
    eg = [jnp.exp(g) for g in gam]
    rhs = [jnp.concatenate([kbi * egi, vi * bi], axis=1).astype(BF16)
           for kbi, egi, vi, bi in zip(kb, eg, v, beta)]
    kdt = [_transpose_chunk(ki * jnp.exp(gli - g)) for ki, gli, g in zip(k, gl, gam)]
    a = [jnp.where(strict[d], bo[:CHUNK] * de, 0.0) for bo, de, d in zip(both, decay, dirs)]
    qk = [bo[CHUNK:] * de for bo, de in zip(both, decay)]
    ab = [ai.astype(BF16) for ai in a]
    x = [jnp.where(eye, 1.0, 0.0) - ai for ai in a]
    pw = [_dot(abi, abi) for abi in ab]
    n_doublings = 5
    for it in range(n_doublings):
        pb = [pi.astype(BF16) for pi in pw]
        if it + 1 < n_doublings:
            xp = [_dot(jnp.concatenate([xi.astype(BF16), pbi], axis=0), pbi) for xi, pbi in zip(x, pb)]
            x = [xi + xpi[:CHUNK] for xi, xpi in zip(x, xp)]
            pw = [xpi[CHUNK:] for xpi in xp]
        else:
            x = [xi + _dot(xi.astype(BF16), pbi) for xi, pbi in zip(x, pb)]
    sol = [_dot(xi.astype(BF16), ri).astype(BF16) for xi, ri in zip(x, rhs)]
    lr = [_dot(jnp.concatenate([kt, qki], axis=0).astype(BF16), si)
          for kt, qki, si in zip(kdt, qk, sol)]
    for (slot, _, _), qi, egi, gli, lri in zip(chains, q, eg, gl, lr):
        w2_ref[slot] = jnp.concatenate(
            [lri[:HEAD_K, :HEAD_K], qi * egi - lri[HEAD_K:, :HEAD_K]], axis=0).astype(BF16)
        nc_ref[slot] = lri[:HEAD_K, HEAD_K:]
        oc_ref[slot] = lri[HEAD_K:, HEAD_K:]
        eg_ref[slot] = jnp.broadcast_to(jnp.exp(gli), (SUBLANES, HEAD_V))


def _gdn_step_pair(o_ref, w2_ref, nc_ref, oc_ref, eg_ref, s_refs, pair):
    s = [s_ref[...] for s_ref in s_refs]
    rm = [_dot(w2_ref[slot], si.astype(BF16)) for (slot, _), si in zip(pair, s)]
    for (slot, r0), s_ref, si, ri in zip(pair, s_refs, s, rm):
        o_ref[pl.ds(r0, CHUNK), :] += ri[HEAD_K:] + oc_ref[slot]
        s_ref[...] = si * eg_ref[slot][0:1, :] + nc_ref[slot] - ri[:HEAD_K]


def _gdn_kernel(alog_ref, dt_ref, qc, kc, vc, zc, smc, qx, kx, vx, zx, smx, wq_ref, wk_ref, wv_ref, on_ref,
                yc_ref, yx_ref, qnc, knc, vvc, parc, qnx, knx, vvx, parx, oc_scr, ox_scr, sf_ref, sb_ref,
                w2_ref, nc_ref, oc_ref, eg_ref):
    lc = qc.shape[1]
    lx = qx.shape[1]
    h = pl.program_id(1)

    row = lax.broadcasted_iota(I32, (N_SMALL, LANES), 0)
    lane = lax.broadcasted_iota(I32, (N_SMALL, LANES), 1)
    pick = ((row == h + N_HEADS * lane) & (lane < 4)).astype(BF16)
    lane1 = lax.broadcasted_iota(I32, (1, LANES), 1)
    alog = jnp.where(lane1 == 0, alog_ref[0, h], alog_ref[1, h])
    dtb = jnp.where(lane1 == 0, dt_ref[0, h], dt_ref[1, h])
    neg_a = -jnp.exp(alog)

    def prep(q_ref, k_ref, v_ref, sm_ref, qn, kn, vv, par, seq, rowlen):
        blk = rowlen if rowlen > CHUNK else min(256, seq)

        def body(r, carry):
            rows = pl.ds(pl.multiple_of(r * blk, blk), blk)
            qn[rows, :] = _l2norm(_conv_silu(q_ref[0, rows, :].astype(F32), wq_ref, rowlen)) * Q_SCALE
            kn[rows, :] = _l2norm(_conv_silu(k_ref[0, rows, :].astype(F32), wk_ref, rowlen))
            vv[rows, :] = _conv_silu(v_ref[0, rows, :].astype(F32), wv_ref, rowlen)
            picked = _dot_exact_rhs01(sm_ref[0, rows, :], pick)
            pv = jnp.where(lane1 < 2, neg_a * _softplus(picked + dtb), jax.nn.sigmoid(picked))
            cum_f = pltpu.roll(_chunk_cumsum(pv, False), 4, 1)
            cum_b = pltpu.roll(_chunk_cumsum(pv, True), 4, 1)
            par[rows, :] = jnp.where(lane1 == 4, cum_f, jnp.where(lane1 == 5, cum_b, pv))
            return carry

        lax.fori_loop(0, seq // blk, body, 0)

    prep(qc, kc, vc, smc, qnc, knc, vvc, parc, lc, lc)
    prep(qx, kx, vx, smx, qnx, knx, vvx, parx, lx, GRID_W)

    r, c = _chunk_masks()
    lower = c <= r
    upper = c >= r
    eye = c == r

    sf_ref[...] = jnp.zeros_like(sf_ref)
    sb_ref[...] = jnp.zeros_like(sb_ref)
    _zero_rows(oc_scr, lc)
    _zero_rows(ox_scr, lx)

    def run(qn, kn, vv, par, o_scr, seq):
        def prep_all(chains):
            _gdn_prep_all(qn, kn, vv, par, w2_ref, nc_ref, oc_ref, eg_ref, chains, eye,
                          (lower, upper), (c < r, c > r))

        def step_pair(fwd, bwd):
            _gdn_step_pair(o_scr, w2_ref, nc_ref, oc_ref, eg_ref, (sf_ref, sb_ref), (fwd, bwd))

        _bidir_blocks(seq // CHUNK, prep_all, step_pair)

    run(qnc, knc, vvc, parc, oc_scr, lc)
    run(qnx, knx, vvx, parx, ox_scr, lx)
    _head_finalize(oc_scr, zc, on_ref, yc_ref, lc)
    _head_finalize(ox_scr, zx, on_ref, yx_ref, lx)


def _gdn_mix(pc, smc, px, smx, conv_w, a_log, dt_bias, onorm):
    bsz, lc, _ = pc.shape
    lx = px.shape[1]
    assert lc % CHUNK == 0 and lx % CHUNK == 0 and lx % GRID_W == 0
    kblk = HK // HEAD_K
    vblk = 2 * HK // HEAD_V
    smem = pl.BlockSpec(memory_space=pltpu.SMEM)
    return pl.pallas_call(
        _gdn_kernel,
        grid=(bsz, N_HEADS),
        in_specs=[smem, smem] + _head_specs(lc) + _head_specs(lx) + [
            pl.BlockSpec((CONV_W, HEAD_K), lambda b, h: (0, h)),
            pl.BlockSpec((CONV_W, HEAD_K), lambda b, h: (0, kblk + h)),
            pl.BlockSpec((CONV_W, HEAD_V), lambda b, h: (0, vblk + h)),
            pl.BlockSpec((1, HEAD_V), lambda b, h: (0, 0)),
        ],
        out_specs=[
            pl.BlockSpec((1, lc, HEAD_V), lambda b, h: (b, 0, h)),
            pl.BlockSpec((1, lx, HEAD_V), lambda b, h: (b, 0, h)),
        ],
        out_shape=[
            jax.ShapeDtypeStruct((bsz, lc, HV), BF16),
            jax.ShapeDtypeStruct((bsz, lx, HV), BF16),
        ],
        scratch_shapes=[
            pltpu.VMEM((lc, HEAD_K), F32), pltpu.VMEM((lc, HEAD_K), F32),
            pltpu.VMEM((lc, HEAD_V), F32), pltpu.VMEM((lc, LANES), F32),
            pltpu.VMEM((lx, HEAD_K), F32), pltpu.VMEM((lx, HEAD_K), F32),
            pltpu.VMEM((lx, HEAD_V), F32), pltpu.VMEM((lx, LANES), F32),
            pltpu.VMEM((lc, HEAD_V), F32), pltpu.VMEM((lx, HEAD_V), F32),
            pltpu.VMEM((HEAD_K, HEAD_V), F32), pltpu.VMEM((HEAD_K, HEAD_V), F32),
            pltpu.VMEM((2 * K_CHUNKS, HEAD_K + CHUNK, HEAD_K), BF16),
            pltpu.VMEM((2 * K_CHUNKS, HEAD_K, HEAD_V), F32),
            pltpu.VMEM((2 * K_CHUNKS, CHUNK, HEAD_V), F32),
            pltpu.VMEM((2 * K_CHUNKS, SUBLANES, HEAD_V), F32),
        ],
        compiler_params=_cparams("parallel", "parallel"),
        name="gdn_bidir",
    )(a_log, dt_bias, pc, pc, pc, pc, smc, px, px, px, px, smx, conv_w, conv_w, conv_w,
      onorm.reshape(1, HEAD_V))


def _outproj_kernel(y_ref, w_ref, x_ref, g_ref, o_ref):
    o_ref[0] = x_ref[0] + g_ref[0] * _dot(y_ref[0], w_ref[...])


def _outproj(y, w, x, mod_l, mod_row, gate_blk):
    bsz, seq, d = x.shape
    kdim = y.shape[2]
    tm = min(512, seq)
    tn = min(1024, d)
    nblk = d // tn
    return pl.pallas_call(
        _outproj_kernel,
        grid=(bsz, seq // tm, nblk),
        in_specs=[
            pl.BlockSpec((1, tm, kdim), lambda b, i, j: (b, i, 0)),
            pl.BlockSpec((kdim, tn), lambda b, i, j: (0, j)),
            pl.BlockSpec((1, tm, tn), lambda b, i, j: (b, i, j)),
            pl.BlockSpec((1, 1, tn), lambda b, i, j: (mod_row(b), 0, gate_blk * nblk + j)),
        ],
        out_specs=pl.BlockSpec((1, tm, tn), lambda b, i, j: (b, i, j)),
        out_shape=jax.ShapeDtypeStruct(x.shape, F32),
        compiler_params=_cparams("parallel", "parallel", "parallel"),
        name="outproj_residual",
    )(y, w, x, mod_l)


def _router_kernel(x_ref, g_ref, sh_ref, sc_ref, whi_ref, wlo_ref, a_ref, hi_ref, lo_ref):
    tm = hi_ref.shape[0]
    slab = _row_slab(tm)
    gv = g_ref[...]
    scale1 = 1.0 + sc_ref[0]
    shift = sh_ref[0]

    def body(r, carry):
        rows = pl.ds(pl.multiple_of(r * slab, slab), slab)
        h = _norm_mod_rows(x_ref[0, rows, :], gv, scale1, shift)
        hi = h.astype(BF16)
        hi_ref[rows, :] = hi
        lo_ref[rows, :] = (h - hi.astype(F32)).astype(BF16)
        return carry

    lax.fori_loop(0, tm // slab, body, 0)
    hi = hi_ref[...]
    w_hi = whi_ref[...]
    logits = (_dot(hi, w_hi) + _dot(hi, wlo_ref[...]) + _dot(lo_ref[...], w_hi)).T[:N_EXPERTS]
    m = jnp.max(logits, axis=0, keepdims=True)
    e = jnp.exp(logits - m)
    a_ref[0] = e / jnp.sum(e, axis=0, keepdims=True)


def _router(x, norm_g, mod_l, mod_row, w_router_hi, w_router_lo):
    bsz, seq, d = x.shape
    tm = min(512, seq)
    return pl.pallas_call(
        _router_kernel,
        grid=(bsz, seq // tm),
        in_specs=[
            pl.BlockSpec((1, tm, d), lambda b, i: (b, i, 0)),
            pl.BlockSpec((1, d), lambda b, i: (0, 0)),
            pl.BlockSpec((1, 1, d), lambda b, i: (mod_row(b), 0, 3)),
            pl.BlockSpec((1, 1, d), lambda b, i: (mod_row(b), 0, 4)),
            pl.BlockSpec((d, LANES), lambda b, i: (0, 0)),
            pl.BlockSpec((d, LANES), lambda b, i: (0, 0)),
        ],
        out_specs=pl.BlockSpec((1, N_EXPERTS, tm), lambda b, i: (b, 0, i)),
        out_shape=jax.ShapeDtypeStruct((bsz, N_EXPERTS, seq), F32),
        scratch_shapes=[pltpu.VMEM((tm, d), BF16), pltpu.VMEM((tm, d), BF16)],
        compiler_params=_cparams("parallel", "parallel"),
        name="router_affinity",
    )(x, norm_g.reshape(1, d), mod_l, mod_l, w_router_hi, w_router_lo)


def _lane_cumsum(src_ref, dst_ref, seq):
    r = lax.broadcasted_iota(I32, (LANES, LANES), 0)
    c = lax.broadcasted_iota(I32, (LANES, LANES), 1)
    upper = (r <= c).astype(BF16)
    carry = jnp.zeros((N_EXPERTS, 1), F32)
    for jb in range(seq // LANES):
        sl = slice(jb * LANES, (jb + 1) * LANES)
        cs = _dot(src_ref[:, sl].astype(BF16), upper) + carry
        dst_ref[:, sl] = cs
        carry = cs[:, LANES - 1:LANES]


def _topk_kernel(a_ref, idx_ref, tok_ref, m_scr, cum_scr, pack_scr, *, cap):
    a = a_ref[0]
    seq = a.shape[1]
    bits = lax.bitcast_convert_type(a, I32)

    def search(i, prefix):
        cand = prefix | jnp.left_shift(jnp.int32(1), 30 - i)
        cnt = jnp.sum((bits >= cand).astype(F32), axis=1, keepdims=True)
        return jnp.where(cnt >= cap, cand, prefix)

    thr = lax.fori_loop(0, 31, search, jnp.zeros((N_EXPERTS, 1), I32))
    gt = bits > thr
    eq = bits == thr
    need = cap - jnp.sum(gt.astype(F32), axis=1, keepdims=True)
    m_scr[...] = eq.astype(F32)
    _lane_cumsum(m_scr, cum_scr, seq)
    sel = gt | (eq & (cum_scr[...] - 1.0 < need))
    m_scr[...] = sel.astype(F32)
    _lane_cumsum(m_scr, cum_scr, seq)

    pack_scr[...] = jnp.zeros_like(pack_scr)
    pack_scr[0:N_EXPERTS, :] = jnp.where(sel, cum_scr[...] - 1.0, -1.0)
    pack_scr[N_EXPERTS:2 * N_EXPERTS, :] = jnp.where(sel, a, 0.0)
    tok_ref[0] = pack_scr[...].T

    cap_pad = idx_ref.shape[2]
    ones = jnp.ones((SUBLANES, LANES), BF16)
    for e in range(N_EXPERTS):
        for pb in range(cap_pad // LANES):
            slot = (lax.broadcasted_iota(I32, (LANES, 1), 0) + pb * LANES).astype(F32)

            def count_block(jb, acc, e=e, slot=slot):
                row = cum_scr[e:e + 1, pl.ds(pl.multiple_of(jb * LANES, LANES), LANES)]
                return acc + (row <= slot).astype(F32)

            acc = lax.fori_loop(0, seq // LANES, count_block, jnp.zeros((LANES, LANES), F32))
            counts = _dot_nt(ones, acc.astype(BF16))
            idx_ref[0, e:e + 1, pb * LANES:(pb + 1) * LANES] = (
                jnp.minimum(counts[0:1, :], seq - 1.0).astype(I32))


def _topk(aff_t, cap, cap_pad):
    bsz, _, seq = aff_t.shape
    return pl.pallas_call(
        functools.partial(_topk_kernel, cap=cap),
        grid=(bsz,),
        in_specs=[pl.BlockSpec((1, N_EXPERTS, seq), lambda b: (b, 0, 0))],
        out_specs=[
            pl.BlockSpec((1, N_EXPERTS, cap_pad), lambda b: (b, 0, 0)),
            pl.BlockSpec((1, seq, LANES), lambda b: (b, 0, 0)),
        ],
        out_shape=[
            jax.ShapeDtypeStruct((bsz, N_EXPERTS, cap_pad), I32),
            jax.ShapeDtypeStruct((bsz, seq, LANES), F32),
        ],
        scratch_shapes=[
            pltpu.VMEM((N_EXPERTS, seq), F32), pltpu.VMEM((N_EXPERTS, seq), F32),
            pltpu.VMEM((LANES, seq), F32),
        ],
        compiler_params=_cparams("parallel"),
        name="expert_choice_select",
    )(aff_t)


def _expert_kernel(idx_ref, x_hbm, g_ref, sh_ref, sc_ref, wg_ref, wu_ref, o_ref, xs_ref, hs_ref, sem, *,
                   cap, cap_pad):
    nb = pl.num_programs(1)
    e = pl.program_id(0)
    b = pl.program_id(1)
    step = e * nb + b
    cur = step % 2
    has_next = step + 1 < pl.num_programs(0) * nb
    next_e = (step + 1) // nb
    next_b = (step + 1) % nb
    slab = _row_slab(cap_pad)
    n_slabs = cap_pad // slab
    assert cap % n_slabs == 0
    per_slab = cap // n_slabs

    def row_copy(buf, bb, p, t):
        return pltpu.make_async_copy(x_hbm.at[bb, pl.ds(t, 1), :], xs_ref.at[buf, pl.ds(p, 1), :], sem.at[buf])

    @pl.when(step == 0)
    def _():
        if cap < cap_pad:
            xs_ref[:, cap:cap_pad, :] = jnp.zeros((2, cap_pad - cap, xs_ref.shape[2]), F32)
        base = (b * N_EXPERTS + e) * cap_pad

        def issue(p, carry):
            row_copy(0, b, p, idx_ref[base + p]).start()
            return carry

        lax.fori_loop(0, cap, issue, 0, unroll=8)

    def drain(p, carry):
        row_copy(cur, b, p, 0).wait()
        return carry

    lax.fori_loop(0, cap, drain, 0, unroll=8)

    gv = g_ref[...]
    scale1 = 1.0 + sc_ref[0]
    shift = sh_ref[0]
    next_base = (next_b * N_EXPERTS + next_e) * cap_pad

    def norm_loop(start_next):
        def body(r, carry):
            if start_next:
                for q in range(per_slab):
                    p = r * per_slab + q
                    row_copy(1 - cur, next_b, p, idx_ref[next_base + p]).start()
            rows = pl.ds(pl.multiple_of(r * slab, slab), slab)
            hs_ref[rows, :] = _norm_mod_rows(xs_ref[cur, rows, :], gv, scale1, shift).astype(BF16)
            return carry

        lax.fori_loop(0, n_slabs, body, 0)

    @pl.when(has_next)
    def _():
        norm_loop(True)

    @pl.when(jnp.logical_not(has_next))
    def _():
        norm_loop(False)

    hs = hs_ref[...]
    o_ref[0, 0] = (_silu(_dot(hs, wg_ref[0])) * _dot(hs, wu_ref[0])).astype(o_ref.dtype)


def _experts(idx, cap, x, norm_g, mod_l, mod_row, w_gate, w_up):
    bsz, seq, d = x.shape
    cap_pad = idx.shape[2]
    grid_spec = pltpu.PrefetchScalarGridSpec(
        num_scalar_prefetch=1,
        grid=(N_EXPERTS, bsz),
        in_specs=[
            pl.BlockSpec(memory_space=pl.ANY),
            pl.BlockSpec((1, d), lambda e, b, idx_ref: (0, 0)),
            pl.BlockSpec((1, 1, d), lambda e, b, idx_ref: (mod_row(b), 0, 3)),
            pl.BlockSpec((1, 1, d), lambda e, b, idx_ref: (mod_row(b), 0, 4)),
            pl.BlockSpec((1, d, EXPERT_FF), lambda e, b, idx_ref: (e, 0, 0)),
            pl.BlockSpec((1, d, EXPERT_FF), lambda e, b, idx_ref: (e, 0, 0)),
        ],
        out_specs=pl.BlockSpec((1, 1, cap_pad, EXPERT_FF), lambda e, b, idx_ref: (b, e, 0, 0)),
        scratch_shapes=[pltpu.VMEM((2, cap_pad, d), F32), pltpu.VMEM((cap_pad, d), BF16),
                        pltpu.SemaphoreType.DMA((2,))],
    )
    return pl.pallas_call(
        functools.partial(_expert_kernel, cap=cap, cap_pad=cap_pad),
        grid_spec=grid_spec,
        out_shape=jax.ShapeDtypeStruct((bsz, N_EXPERTS, cap_pad, EXPERT_FF), BF16),
        compiler_params=_cparams("arbitrary", "arbitrary"),
        name="expert_gather_swiglu",
    )(idx.reshape(-1), x, norm_g.reshape(1, d), mod_l, mod_l, w_gate, w_up)


def _down_kernel(tok_ref, hid_ref, wd_ref, x_ref, g_ref, o_ref, hcat_ref):
    tm = hcat_ref.shape[0]
    cap_pad = hid_ref.shape[2]

    @pl.when(pl.program_id(2) == 0)
    def _():
        tok = tok_ref[0]
        slot = lax.broadcasted_iota(I32, (tm, cap_pad), 1).astype(F32)
        for e in range(N_EXPERTS):
            onehot = (tok[:, e:e + 1] == slot).astype(BF16)
            rows = _dot(onehot, hid_ref[0, e])
            hcat_ref[:, e * EXPERT_FF:(e + 1) * EXPERT_FF] = (
                rows * tok[:, N_EXPERTS + e:N_EXPERTS + e + 1]).astype(BF16)

    o_ref[0] = x_ref[0] + g_ref[0] * _dot(hcat_ref[...], wd_ref[...])


def _combine_down(tok, hid, w_down_cat, x, mod_l, mod_row):
    bsz, seq, d = x.shape
    cap_pad = hid.shape[2]
    kdim = N_EXPERTS * EXPERT_FF
    tm = min(512, seq)
    tn = min(1024, d)
    nblk = d // tn
    return pl.pallas_call(
        _down_kernel,
        grid=(bsz, seq // tm, nblk),
        in_specs=[
            pl.BlockSpec((1, tm, LANES), lambda b, i, j: (b, i, 0)),
            pl.BlockSpec((1, N_EXPERTS, cap_pad, EXPERT_FF), lambda b, i, j: (b, 0, 0, 0)),
            pl.BlockSpec((kdim, tn), lambda b, i, j: (0, j)),
            pl.BlockSpec((1, tm, tn), lambda b, i, j: (b, i, j)),
            pl.BlockSpec((1, 1, tn), lambda b, i, j: (mod_row(b), 0, 5 * nblk + j)),
        ],
        out_specs=pl.BlockSpec((1, tm, tn), lambda b, i, j: (b, i, j)),
        out_shape=jax.ShapeDtypeStruct(x.shape, F32),
        scratch_shapes=[pltpu.VMEM((tm, kdim), BF16)],
        compiler_params=_cparams("parallel", "parallel", "arbitrary"),
        name="combine_down_residual",
    )(tok, hid, w_down_cat, x, mod_l)


def _ec_ffn_residual(x, norm_g, mod_l, mod_row, w_router, w_gate, w_up, w_down_cat):
    seq = x.shape[1]
    cap = CAPACITY_FACTOR * seq // N_EXPERTS
    cap_pad = -(-cap // LANES) * LANES
    aff_t = _router(x, norm_g, mod_l, mod_row, *w_router)
    idx, tok = _topk(aff_t, cap, cap_pad)
    hid = _experts(idx, cap, x, norm_g, mod_l, mod_row, w_gate, w_up)
    return _combine_down(tok, hid, w_down_cat, x, mod_l, mod_row)


def _final_kernel(x_ref, g_ref, o_ref):
    tm = x_ref.shape[1]
    slab = _row_slab(tm)
    gv = g_ref[...]

    def body(r, carry):
        rows = pl.ds(pl.multiple_of(r * slab, slab), slab)
        x = x_ref[0, rows, :]
        ms = jnp.mean(x * x, axis=-1, keepdims=True)
        o_ref[0, rows, :] = x * lax.rsqrt(ms + EPS) * gv
        return carry

    lax.fori_loop(0, tm // slab, body, 0)


def _final_norm(x, g):
    bsz, seq, d = x.shape
    tm = min(512, seq)
    return pl.pallas_call(
        _final_kernel,
        grid=(bsz, seq // tm),
        in_specs=[
            pl.BlockSpec((1, tm, d), lambda b, i: (b, i, 0)),
            pl.BlockSpec((1, d), lambda b, i: (0, 0)),
        ],
        out_specs=pl.BlockSpec((1, tm, d), lambda b, i: (b, i, 0)),
        out_shape=jax.ShapeDtypeStruct(x.shape, F32),
        compiler_params=_cparams("parallel", "parallel"),
        name="final_rmsnorm",
    )(x, g.reshape(1, d))


def _split_inproj_weight(w_in):
    w_main = w_in[:, :N_MAIN].astype(BF16)
    w_small = w_in[:, N_MAIN:]
    w_small = jnp.pad(w_small, ((0, 0), (0, N_SMALL - w_small.shape[1]))).astype(BF16)
    return w_main, w_small


def kernel(x, c, ctx, c_ctx, ada_w, ada_b, norm1_g, norm2_g, gla_w_in, gla_wg2, gla_bg2, gla_onorm, gla_w_out, gdn_w_in, gdn_conv, gdn_A_log, gdn_dt_bias, gdn_onorm, gdn_w_out, router_w, exp_w_gate, exp_w_up, exp_w_down, final_g):
    bsz, _, d = x.shape
    depth = ada_w.shape[0]
    n_rows = -(-(bsz + 1) // SUBLANES) * SUBLANES
    c_rows = jnp.zeros((n_rows, d), F32).at[:bsz].set(c).at[bsz].set(c_ctx)
    mod_l = _modulation(c_rows, ada_w, ada_b).reshape(depth * n_rows, 1, 6 * d)

    cx = ctx
    for i in range(depth):
        last = i == depth - 1
        j = i // 2

        def lat_row(b, i=i):
            return i * n_rows + b

        def ctx_row(b, i=i):
            return i * n_rows + bsz

        if i % 2 == 0:
            w_main, w_small = _split_inproj_weight(gla_w_in[j])
            w_out = gla_w_out[j].astype(BF16)
        else:
            w_main, w_small = _split_inproj_weight(gdn_w_in[j])
            w_out = gdn_w_out[j].astype(BF16)
        pc, smc = _inproj(cx, norm1_g[i], mod_l, ctx_row, w_main, w_small)
        px, smx = _inproj(x, norm1_g[i], mod_l, lat_row, w_main, w_small)
        if i % 2 == 0:
            wg_pad = jnp.zeros((2, N_SMALL, HK), F32)
            wg_pad = wg_pad.at[0, :GLA_LOWRANK].set(gla_wg2[j, 0])
            wg_pad = wg_pad.at[1, GLA_LOWRANK:2 * GLA_LOWRANK].set(gla_wg2[j, 1]).astype(BF16)
            yc, yx = _gla_mix(pc, smc, px, smx, wg_pad, gla_bg2[j].reshape(2, 1, HK), gla_onorm[j])
        else:
            yc, yx = _gdn_mix(pc, smc, px, smx, gdn_conv[j], gdn_A_log[j], gdn_dt_bias[j], gdn_onorm[j])
        w_r = jnp.pad(router_w[i], ((0, 0), (0, LANES - N_EXPERTS)))
        w_r_hi = w_r.astype(BF16)
        w_router = (w_r_hi, (w_r - w_r_hi.astype(F32)).astype(BF16))
        w_gate = exp_w_gate[i].astype(BF16)
        w_up = exp_w_up[i].astype(BF16)
        w_down_cat = exp_w_down[i].reshape(N_EXPERTS * EXPERT_FF, d).astype(BF16)
        x = _outproj(yx, w_out, x, mod_l, lat_row, 2)
        x = _ec_ffn_residual(x, norm2_g[i], mod_l, lat_row, w_router, w_gate, w_up, w_down_cat)
        if not last:
            cx = _outproj(yc, w_out, cx, mod_l, ctx_row, 2)
            cx = _ec_ffn_residual(cx, norm2_g[i], mod_l, ctx_row, w_router, w_gate, w_up, w_down_cat)
    return _final_norm(x, final_g)
```

```python
import functools

import jax
import jax.numpy as jnp
from jax import lax
from jax.experimental import pallas as pl
from jax.experimental.pallas import tpu as pltpu

F32 = jnp.float32
BF16 = jnp.bfloat16
I32 = jnp.int32

N_HEADS = 16
HEAD_K = 128
HEAD_V = 256
HK = N_HEADS * HEAD_K
HV = N_HEADS * HEAD_V
CHUNK = 64
GRID_W = 64
GLA_LOWRANK = 16
GLA_GATE_NORM = 16.0
CONV_W = 5
N_EXPERTS = 16
EXPERT_FF = 256
CAPACITY_FACTOR = 2
EPS = 1e-6
N_MAIN = 2 * HK + 2 * HV
N_SMALL = 128
LANES = 128
SUBLANES = 8
VMEM_LIMIT_BYTES = 56 * 1024 * 1024
Q_SCALE = HEAD_K ** -0.5
K_CHUNKS = 8


def _cparams(*sem):
    return pltpu.CompilerParams(dimension_semantics=sem, vmem_limit_bytes=VMEM_LIMIT_BYTES)


def _dot(a, b):
    return jnp.dot(a, b, preferred_element_type=F32)


def _dot_nt(a, b):
    return lax.dot_general(a, b, (((1,), (1,)), ((), ())), preferred_element_type=F32)


def _silu(x):
    return x * jax.nn.sigmoid(x)


def _log_sigmoid(x):
    return jnp.minimum(x, 0.0) - jnp.log1p(jnp.exp(-jnp.abs(x)))


def _softplus(x):
    return jnp.maximum(x, 0.0) + jnp.log1p(jnp.exp(-jnp.abs(x)))


def _split3(x):
    hi = x.astype(BF16)
    r = x - hi.astype(F32)
    mid = r.astype(BF16)
    lo = (r - mid.astype(F32)).astype(BF16)
    return hi, mid, lo


def _dot_exact_rhs01(x, m01):
    hi, mid, lo = _split3(x)
    return _dot(hi, m01) + _dot(mid, m01) + _dot(lo, m01)


def _chunk_cumsum(x, reverse):
    n = x.shape[0]
    t = lax.broadcasted_iota(I32, (n, 1), 0) % CHUNK
    s = 1
    while s < CHUNK:
        if reverse:
            shifted = pltpu.roll(x, n - s, 0)
            valid = t < CHUNK - s
        else:
            shifted = pltpu.roll(x, s, 0)
            valid = t >= s
        x = x + jnp.where(valid, shifted, 0.0)
        s *= 2
    return x


def _transpose_chunk(x):
    pad = jnp.zeros((LANES - CHUNK, x.shape[1]), F32)
    return jnp.concatenate([x, pad], axis=0).T[:, :CHUNK]


def _chunk_masks():
    r = lax.broadcasted_iota(I32, (CHUNK, CHUNK), 0)
    c = lax.broadcasted_iota(I32, (CHUNK, CHUNK), 1)
    return r, c


def _mod_kernel(c_ref, w_ref, b_ref, o_ref, acc_ref):
    k = pl.program_id(2)

    @pl.when(k == 0)
    def _():
        acc_ref[...] = jnp.zeros_like(acc_ref)

    acc_ref[...] += _dot(_silu(c_ref[...]).astype(BF16), w_ref[0].astype(BF16))

    @pl.when(k == pl.num_programs(2) - 1)
    def _():
        o_ref[0] = acc_ref[...] + b_ref[0]


def _modulation(c_rows, ada_w, ada_b):
    depth, d, n = ada_w.shape
    rows = c_rows.shape[0]
    tk = min(1024, d)
    tn = min(2048, n)
    return pl.pallas_call(
        _mod_kernel,
        grid=(depth, n // tn, d // tk),
        in_specs=[
            pl.BlockSpec((rows, tk), lambda l, j, k: (0, k)),
            pl.BlockSpec((1, tk, tn), lambda l, j, k: (l, k, j)),
            pl.BlockSpec((1, 1, tn), lambda l, j, k: (l, 0, j)),
        ],
        out_specs=pl.BlockSpec((1, rows, tn), lambda l, j, k: (l, 0, j)),
        out_shape=jax.ShapeDtypeStruct((depth, rows, n), F32),
        scratch_shapes=[pltpu.VMEM((rows, tn), F32)],
        compiler_params=_cparams("parallel", "parallel", "arbitrary"),
        name="adaln_modulation",
    )(c_rows, ada_w, ada_b.reshape(depth, 1, n))


def _norm_mod_rows(x, gv, scale1, shift):
    ms = jnp.mean(x * x, axis=-1, keepdims=True)
    return (x * lax.rsqrt(ms + EPS) * gv) * scale1 + shift


def _row_slab(tm):
    return 64 if tm % 64 == 0 else tm


def _inproj_kernel(x_ref, g_ref, sh_ref, sc_ref, w_ref, ws_ref, o_ref, os_ref, h_ref):
    tm = h_ref.shape[0]
    slab = _row_slab(tm)

    @pl.when(pl.program_id(2) == 0)
    def _():
        gv = g_ref[...]
        scale1 = 1.0 + sc_ref[0]
        shift = sh_ref[0]

        def body(r, carry):
            rows = pl.ds(pl.multiple_of(r * slab, slab), slab)
            h_ref[rows, :] = _norm_mod_rows(x_ref[0, rows, :], gv, scale1, shift).astype(BF16)
            return carry

        lax.fori_loop(0, tm // slab, body, 0)
        os_ref[0] = _dot(h_ref[...], ws_ref[0])

    o_ref[0] = _dot(h_ref[...], w_ref[0]).astype(o_ref.dtype)


def _inproj(x, norm_g, mod_l, mod_row, w_main, w_small, layer):
    bsz, seq, d = x.shape
    n = w_main.shape[2]
    tm = min(512, seq)
    tn = min(1024, n)
    return pl.pallas_call(
        _inproj_kernel,
        grid=(bsz, seq // tm, n // tn),
        in_specs=[
            pl.BlockSpec((1, tm, d), lambda b, i, j: (b, i, 0)),
            pl.BlockSpec((1, d), lambda b, i, j: (0, 0)),
            pl.BlockSpec((1, 1, d), lambda b, i, j: (mod_row(b), 0, 0)),
            pl.BlockSpec((1, 1, d), lambda b, i, j: (mod_row(b), 0, 1)),
            pl.BlockSpec((1, d, tn), lambda b, i, j: (layer, 0, j)),
            pl.BlockSpec((1, d, N_SMALL), lambda b, i, j: (layer, 0, 0)),
        ],
        out_specs=[
            pl.BlockSpec((1, tm, tn), lambda b, i, j: (b, i, j)),
            pl.BlockSpec((1, tm, N_SMALL), lambda b, i, j: (b, i, 0)),
        ],
        out_shape=[
            jax.ShapeDtypeStruct((bsz, seq, n), BF16),
            jax.ShapeDtypeStruct((bsz, seq, N_SMALL), F32),
        ],
        scratch_shapes=[pltpu.VMEM((tm, d), BF16)],
        compiler_params=_cparams("parallel", "parallel", "arbitrary"),
        name="norm_inproj",
    )(x, norm_g.reshape(1, d), mod_l, mod_l, w_main, w_small)


def _bidir_blocks(n_chunks, prep_all, step_pair):
    kc = min(K_CHUNKS, n_chunks)
    assert n_chunks % kc == 0
    nblk = n_chunks // kc

    def body(i, carry):
        fbase = pl.multiple_of(i * (kc * CHUNK), kc * CHUNK)
        bbase = pl.multiple_of((nblk - 1 - i) * (kc * CHUNK), kc * CHUNK)
        prep_all([(c, fbase + c * CHUNK, 0) for c in range(kc)]
                 + [(kc + c, bbase + c * CHUNK, 1) for c in range(kc)])
        for c in range(kc):
            step_pair((c, fbase + c * CHUNK), (2 * kc - 1 - c, bbase + (kc - 1 - c) * CHUNK))
        return carry

    lax.fori_loop(0, nblk, body, 0)


def _zero_rows(ref, seq):
    blk = min(256, seq)

    def body(r, carry):
        ref[pl.ds(pl.multiple_of(r * blk, blk), blk), :] = jnp.zeros((blk, ref.shape[1]), ref.dtype)
        return carry

    lax.fori_loop(0, seq // blk, body, 0)


def _gla_prep_all(q_ref, k_ref, v_ref, b_refs, o_ref, qe_ref, st_ref, nc_ref, chains, masks):
    last_row = (CHUNK - 1, 0)
    rows = [pl.ds(r0, CHUNK) for _, r0, _ in chains]
    dirs = [d for _, _, d in chains]
    q = [q_ref[0, r, :].astype(F32) * Q_SCALE for r in rows]
    k = [k_ref[0, r, :].astype(F32) for r in rows]
    vb = [v_ref[0, r, :] for r in rows]
    b = [b_refs[d][r, :] for r, d in zip(rows, dirs)]
    btot = [bi[last_row[d]:last_row[d] + 1, :] for bi, d in zip(b, dirs)]
    ref = [bi[CHUNK // 2:CHUNK // 2 + 1, :] for bi in b]
    qt = [(qi * jnp.exp(bi - ri)).astype(BF16) for qi, bi, ri in zip(q, b, ref)]
    kt = [(ki * jnp.exp(ri - bi)).astype(BF16) for ki, bi, ri in zip(k, b, ref)]
    scores = [_dot_nt(qi, ki) for qi, ki in zip(qt, kt)]
    stacked = [jnp.concatenate([ki * jnp.exp(bt - bi), jnp.broadcast_to(jnp.exp(bt), (LANES - CHUNK, HEAD_K))],
                               axis=0) for ki, bi, bt in zip(k, b, btot)]
    st = [s.T for s in stacked]
    att = [jnp.where(masks[d], sc, 0.0).astype(BF16) for sc, d in zip(scores, dirs)]
    intra = [_dot(ai, vi) for ai, vi in zip(att, vb)]
    inc = [_dot(si[:, :CHUNK].astype(BF16), vi) for si, vi in zip(st, vb)]
    for (slot, _, _), r, qi, bi, si, ni, oi in zip(chains, rows, q, b, st, inc, intra):
        qe_ref[slot] = (qi * jnp.exp(bi)).astype(BF16)
        st_ref[slot] = si
        nc_ref[slot] = ni
        o_ref[r, :] += oi


def _gla_step_pair(o_ref, qe_ref, st_ref, nc_ref, s_refs, pair):
    s = [s_ref[...] for s_ref in s_refs]
    inter = [_dot(qe_ref[slot], si.astype(BF16)) for (slot, _), si in zip(pair, s)]
    for (slot, r0), s_ref, si, oi in zip(pair, s_refs, s, inter):
        o_ref[pl.ds(r0, CHUNK), :] += oi
        s_ref[...] = si * st_ref[slot][:, CHUNK:CHUNK + 1] + nc_ref[slot]


def _head_finalize(o_scr, z_ref, on_ref, y_ref, seq):
    blk = min(256, seq)
    onv = on_ref[...]

    def body(r, carry):
        rows = pl.ds(pl.multiple_of(r * blk, blk), blk)
        o = o_scr[rows, :]
        ms = jnp.mean(o * o, axis=-1, keepdims=True)
        y = o * lax.rsqrt(ms + EPS) * onv
        z = z_ref[0, rows, :].astype(F32)
        y_ref[0, rows, :] = (y * _silu(z)).astype(y_ref.dtype)
        return carry

    lax.fori_loop(0, seq // blk, body, 0)


def _gla_kernel(qc, kc, vc, zc, smc, qx, kx, vx, zx, smx, wg_ref, bg_ref, on_ref,
                yc_ref, yx_ref, bfc, bbc, bfx, bbx, oc_scr, ox_scr, sf_ref, sb_ref, qe_ref, st_ref, nc_ref):
    lc = qc.shape[1]
    lx = qx.shape[1]

    def gates(sm_ref, bf_ref, bb_ref, seq):
        blk = min(256, seq)

        def body(r, carry):
            rows = pl.ds(pl.multiple_of(r * blk, blk), blk)
            s = sm_ref[0, rows, :].astype(BF16)
            gf = _log_sigmoid(_dot(s, wg_ref[0]) + bg_ref[0]) * (1.0 / GLA_GATE_NORM)
            gb = _log_sigmoid(_dot(s, wg_ref[1]) + bg_ref[1]) * (1.0 / GLA_GATE_NORM)
            bf_ref[rows, :] = _chunk_cumsum(gf, False)
            bb_ref[rows, :] = _chunk_cumsum(gb, True)
            return carry

        lax.fori_loop(0, seq // blk, body, 0)

    gates(smc, bfc, bbc, lc)
    gates(smx, bfx, bbx, lx)

    r, c = _chunk_masks()
    lower = c <= r
    upper = c >= r

    sf_ref[...] = jnp.zeros_like(sf_ref)
    sb_ref[...] = jnp.zeros_like(sb_ref)
    _zero_rows(oc_scr, lc)
    _zero_rows(ox_scr, lx)

    def run(q_ref, k_ref, v_ref, bf_ref, bb_ref, o_scr, seq):
        def prep_all(chains):
            _gla_prep_all(q_ref, k_ref, v_ref, (bf_ref, bb_ref), o_scr, qe_ref, st_ref, nc_ref, chains,
                          (lower, upper))

        def step_pair(fwd, bwd):
            _gla_step_pair(o_scr, qe_ref, st_ref, nc_ref, (sf_ref, sb_ref), (fwd, bwd))

        _bidir_blocks(seq // CHUNK, prep_all, step_pair)

    run(qc, kc, vc, bfc, bbc, oc_scr, lc)
    run(qx, kx, vx, bfx, bbx, ox_scr, lx)
    _head_finalize(oc_scr, zc, on_ref, yc_ref, lc)
    _head_finalize(ox_scr, zx, on_ref, yx_ref, lx)


def _head_specs(seq):
    kblk = HK // HEAD_K
    vblk = 2 * HK // HEAD_V
    zblk = (2 * HK + HV) // HEAD_V
    return [
        pl.BlockSpec((1, seq, HEAD_K), lambda b, h: (b, 0, h)),
        pl.BlockSpec((1, seq, HEAD_K), lambda b, h: (b, 0, kblk + h)),
        pl.BlockSpec((1, seq, HEAD_V), lambda b, h: (b, 0, vblk + h)),
        pl.BlockSpec((1, seq, HEAD_V), lambda b, h: (b, 0, zblk + h)),
        pl.BlockSpec((1, seq, N_SMALL), lambda b, h: (b, 0, 0)),
    ]


def _gla_mix(pc, smc, px, smx, wg_pad, bg, onorm):
    bsz, lc, _ = pc.shape
    lx = px.shape[1]
    assert lc % CHUNK == 0 and lx % CHUNK == 0
    return pl.pallas_call(
        _gla_kernel,
        grid=(bsz, N_HEADS),
        in_specs=_head_specs(lc) + _head_specs(lx) + [
            pl.BlockSpec((2, N_SMALL, HEAD_K), lambda b, h: (0, 0, h)),
            pl.BlockSpec((2, 1, HEAD_K), lambda b, h: (0, 0, h)),
            pl.BlockSpec((1, HEAD_V), lambda b, h: (0, 0)),
        ],
        out_specs=[
            pl.BlockSpec((1, lc, HEAD_V), lambda b, h: (b, 0, h)),
            pl.BlockSpec((1, lx, HEAD_V), lambda b, h: (b, 0, h)),
        ],
        out_shape=[
            jax.ShapeDtypeStruct((bsz, lc, HV), BF16),
            jax.ShapeDtypeStruct((bsz, lx, HV), BF16),
        ],
        scratch_shapes=[
            pltpu.VMEM((lc, HEAD_K), F32), pltpu.VMEM((lc, HEAD_K), F32),
            pltpu.VMEM((lx, HEAD_K), F32), pltpu.VMEM((lx, HEAD_K), F32),
            pltpu.VMEM((lc, HEAD_V), F32), pltpu.VMEM((lx, HEAD_V), F32),
            pltpu.VMEM((HEAD_K, HEAD_V), F32), pltpu.VMEM((HEAD_K, HEAD_V), F32),
            pltpu.VMEM((2 * K_CHUNKS, CHUNK, HEAD_K), BF16),
            pltpu.VMEM((2 * K_CHUNKS, HEAD_K, LANES), F32),
            pltpu.VMEM((2 * K_CHUNKS, HEAD_K, HEAD_V), F32),
        ],
        compiler_params=_cparams("parallel", "parallel"),
        name="gla_bidir",
    )(pc, pc, pc, pc, smc, px, px, px, px, smx, wg_pad, bg, onorm.reshape(1, HEAD_V))


def _conv_silu(u, w_ref, rowlen):
    n = u.shape[0]
    t = lax.broadcasted_iota(I32, (n, 1), 0) % rowlen
    acc = u * w_ref[CONV_W // 2:CONV_W // 2 + 1, :]
    for j in range(CONV_W):
        d = j - CONV_W // 2
        if d == 0:
            continue
        shifted = pltpu.roll(u, (-d) % n, 0)
        valid = (t + d >= 0) & (t + d < rowlen)
        acc = acc + jnp.where(valid, shifted, 0.0) * w_ref[j:j + 1, :]
    return _silu(acc)


def _l2norm(u):
    return u * lax.rsqrt(jnp.sum(u * u, axis=-1, keepdims=True) + EPS)


def _gdn_prep_all(qn, kn, vv, par, w2_ref, nc_ref, oc_ref, eg_ref, chains, eye, incl, strict):
    last_row = (CHUNK - 1, 0)
    rows = [pl.ds(r0, CHUNK) for _, r0, _ in chains]
    dirs = [d for _, _, d in chains]
    q = [qn[r, :] for r in rows]
    k = [kn[r, :] for r in rows]
    v = [vv[r, :] for r in rows]
    p = [par[r, :] for r in rows]
    gam = [pi[:, 4 + d:5 + d] for pi, d in zip(p, dirs)]
    beta = [pi[:, 2 + d:3 + d] for pi, d in zip(p, dirs)]
    gl = [g[last_row[d]:last_row[d] + 1, :] for g, d in zip(gam, dirs)]
    kb = [ki * bi for ki, bi in zip(k, beta)]
    kbf = [ki.astype(BF16) for ki in k]
    both = [_dot_nt(jnp.concatenate([kbi, qi], axis=0).astype(BF16), kf)
            for kbi, qi, kf in zip(kb, q, kbf)]
    decay = []
    for g, d in zip(gam, dirs):
        gam_c = jnp.broadcast_to(g, (CHUNK, CHUNK))
        gam_r = jnp.sum(jnp.where(eye, gam_c, 0.0), axis=0, keepdims=True)
        decay.append(jnp.exp(jnp.where(incl[d], gam_c - gam_r, -jnp.inf)))
    eg = [jnp.exp(g) for g in gam]
    rhs = [jnp.concatenate([kbi * egi, vi * bi], axis=1).astype(BF16)
           for kbi, egi, vi, bi in zip(kb, eg, v, beta)]
    kdt = [_transpose_chunk(ki * jnp.exp(gli - g)) for ki, gli, g in zip(k, gl, gam)]
    a = [jnp.where(strict[d], bo[:CHUNK] * de, 0.0) for bo, de, d in zip(both, decay, dirs)]
    qk = [bo[CHUNK:] * de for bo, de in zip(both, decay)]
    ab = [ai.astype(BF16) for ai in a]
    x = [jnp.where(eye, 1.0, 0.0) - ai for ai in a]
    pw = [_dot(abi, abi) for abi in ab]
    n_doublings = 5
    for it in range(n_doublings):
        pb = [pi.astype(BF16) for pi in pw]
        if it + 1 < n_doublings:
            xp = [_dot(jnp.concatenate([xi.astype(BF16), pbi], axis=0), pbi) for xi, pbi in zip(x, pb)]
            x = [xi + xpi[:CHUNK] for xi, xpi in zip(x, xp)]
            pw = [xpi[CHUNK:] for xpi in xp]
        else:
            x = [xi + _dot(xi.astype(BF16), pbi) for xi, pbi in zip(x, pb)]
    sol = [_dot(xi.astype(BF16), ri).astype(BF16) for xi, ri in zip(x, rhs)]
    lr = [_dot(jnp.concatenate([kt, qki], axis=0).astype(BF16), si)
          for kt, qki, si in zip(kdt, qk, sol)]
    for (slot, _, _), qi, egi, gli, lri in zip(chains, q, eg, gl, lr):
        w2_ref[slot] = jnp.concatenate(
            [lri[:HEAD_K, :HEAD_K], qi * egi - lri[HEAD_K:, :HEAD_K]], axis=0).astype(BF16)
        nc_ref[slot] = lri[:HEAD_K, HEAD_K:]
        oc_ref[slot] = lri[HEAD_K:, HEAD_K:]
        eg_ref[slot] = jnp.broadcast_to(jnp.exp(gli), (SUBLANES, HEAD_V))


def _gdn_step_pair(o_ref, w2_ref, nc_ref, oc_ref, eg_ref, s_refs, pair):
    s = [s_ref[...] for s_ref in s_refs]
    rm = [_dot(w2_ref[slot], si.astype(BF16)) for (slot, _), si in zip(pair, s)]
    for (slot, r0), s_ref, si, ri in zip(pair, s_refs, s, rm):
        o_ref[pl.ds(r0, CHUNK), :] += ri[HEAD_K:] + oc_ref[slot]
        s_ref[...] = si * eg_ref[slot][0:1, :] + nc_ref[slot] - ri[:HEAD_K]


def _gdn_kernel(alog_ref, dt_ref, qc, kc, vc, zc, smc, qx, kx, vx, zx, smx, wq_ref, wk_ref, wv_ref, on_ref,
                yc_ref, yx_ref, qnc, knc, vvc, parc, qnx, knx, vvx, parx, oc_scr, ox_scr, sf_ref, sb_ref,
                w2_ref, nc_ref, oc_ref, eg_ref):
    lc = qc.shape[1]
    lx = qx.shape[1]
    h = pl.program_id(1)

    row = lax.broadcasted_iota(I32, (N_SMALL, LANES), 0)
    lane = lax.broadcasted_iota(I32, (N_SMALL, LANES), 1)
    pick = ((row == h + N_HEADS * lane) & (lane < 4)).astype(BF16)
    lane1 = lax.broadcasted_iota(I32, (1, LANES), 1)
    alog = jnp.where(lane1 == 0, alog_ref[0, h], alog_ref[1, h])
    dtb = jnp.where(lane1 == 0, dt_ref[0, h], dt_ref[1, h])
    neg_a = -jnp.exp(alog)

    def prep(q_ref, k_ref, v_ref, sm_ref, qn, kn, vv, par, seq, rowlen):
        blk = rowlen if rowlen > CHUNK else min(256, seq)

        def body(r, carry):
            rows = pl.ds(pl.multiple_of(r * blk, blk), blk)
            qn[rows, :] = _l2norm(_conv_silu(q_ref[0, rows, :].astype(F32), wq_ref, rowlen)) * Q_SCALE
            kn[rows, :] = _l2norm(_conv_silu(k_ref[0, rows, :].astype(F32), wk_ref, rowlen))
            vv[rows, :] = _conv_silu(v_ref[0, rows, :].astype(F32), wv_ref, rowlen)
            picked = _dot_exact_rhs01(sm_ref[0, rows, :], pick)
            pv = jnp.where(lane1 < 2, neg_a * _softplus(picked + dtb), jax.nn.sigmoid(picked))
            cum_f = pltpu.roll(_chunk_cumsum(pv, False), 4, 1)
            cum_b = pltpu.roll(_chunk_cumsum(pv, True), 4, 1)
            par[rows, :] = jnp.where(lane1 == 4, cum_f, jnp.where(lane1 == 5, cum_b, pv))
            return carry

        lax.fori_loop(0, seq // blk, body, 0)

    prep(qc, kc, vc, smc, qnc, knc, vvc, parc, lc, lc)
    prep(qx, kx, vx, smx, qnx, knx, vvx, parx, lx, GRID_W)

    r, c = _chunk_masks()
    lower = c <= r
    upper = c >= r
    eye = c == r

    sf_ref[...] = jnp.zeros_like(sf_ref)
    sb_ref[...] = jnp.zeros_like(sb_ref)
    _zero_rows(oc_scr, lc)
    _zero_rows(ox_scr, lx)

    def run(qn, kn, vv, par, o_scr, seq):
        def prep_all(chains):
            _gdn_prep_all(qn, kn, vv, par, w2_ref, nc_ref, oc_ref, eg_ref, chains, eye,
                          (lower, upper), (c < r, c > r))

        def step_pair(fwd, bwd):
            _gdn_step_pair(o_scr, w2_ref, nc_ref, oc_ref, eg_ref, (sf_ref, sb_ref), (fwd, bwd))

        _bidir_blocks(seq // CHUNK, prep_all, step_pair)

    run(qnc, knc, vvc, parc, oc_scr, lc)
    run(qnx, knx, vvx, parx, ox_scr, lx)
    _head_finalize(oc_scr, zc, on_ref, yc_ref, lc)
    _head_finalize(ox_scr, zx, on_ref, yx_ref, lx)


def _gdn_mix(pc, smc, px, smx, conv_w, a_log, dt_bias, onorm):
    bsz, lc, _ = pc.shape
    lx = px.shape[1]
    assert lc % CHUNK == 0 and lx % CHUNK == 0 and lx % GRID_W == 0
    kblk = HK // HEAD_K
    vblk = 2 * HK // HEAD_V
    smem = pl.BlockSpec(memory_space=pltpu.SMEM)
    return pl.pallas_call(
        _gdn_kernel,
        grid=(bsz, N_HEADS),
        in_specs=[smem, smem] + _head_specs(lc) + _head_specs(lx) + [
            pl.BlockSpec((CONV_W, HEAD_K), lambda b, h: (0, h)),
            pl.BlockSpec((CONV_W, HEAD_K), lambda b, h: (0, kblk + h)),
            pl.BlockSpec((CONV_W, HEAD_V), lambda b, h: (0, vblk + h)),
            pl.BlockSpec((1, HEAD_V), lambda b, h: (0, 0)),
        ],
        out_specs=[
            pl.BlockSpec((1, lc, HEAD_V), lambda b, h: (b, 0, h)),
            pl.BlockSpec((1, lx, HEAD_V), lambda b, h: (b, 0, h)),
        ],
        out_shape=[
            jax.ShapeDtypeStruct((bsz, lc, HV), BF16),
            jax.ShapeDtypeStruct((bsz, lx, HV), BF16),
        ],
        scratch_shapes=[
            pltpu.VMEM((lc, HEAD_K), F32), pltpu.VMEM((lc, HEAD_K), F32),
            pltpu.VMEM((lc, HEAD_V), F32), pltpu.VMEM((lc, LANES), F32),
            pltpu.VMEM((lx, HEAD_K), F32), pltpu.VMEM((lx, HEAD_K), F32),
            pltpu.VMEM((lx, HEAD_V), F32), pltpu.VMEM((lx, LANES), F32),
            pltpu.VMEM((lc, HEAD_V), F32), pltpu.VMEM((lx, HEAD_V), F32),
            pltpu.VMEM((HEAD_K, HEAD_V), F32), pltpu.VMEM((HEAD_K, HEAD_V), F32),
            pltpu.VMEM((2 * K_CHUNKS, HEAD_K + CHUNK, HEAD_K), BF16),
            pltpu.VMEM((2 * K_CHUNKS, HEAD_K, HEAD_V), F32),
            pltpu.VMEM((2 * K_CHUNKS, CHUNK, HEAD_V), F32),
            pltpu.VMEM((2 * K_CHUNKS, SUBLANES, HEAD_V), F32),
        ],
        compiler_params=_cparams("parallel", "parallel"),
        name="gdn_bidir",
    )(a_log, dt_bias, pc, pc, pc, pc, smc, px, px, px, px, smx, conv_w, conv_w, conv_w,
      onorm.reshape(1, HEAD_V))


def _outproj_kernel(y_ref, w_ref, x_ref, g_ref, o_ref):
    o_ref[0] = x_ref[0] + g_ref[0] * _dot(y_ref[0], w_ref[0])


def _outproj(y, w, layer, x, mod_l, mod_row, gate_blk):
    bsz, seq, d = x.shape
    kdim = y.shape[2]
    tm = min(512, seq)
    tn = min(1024, d)
    nblk = d // tn
    return pl.pallas_call(
        _outproj_kernel,
        grid=(bsz, seq // tm, nblk),
        in_specs=[
            pl.BlockSpec((1, tm, kdim), lambda b, i, j: (b, i, 0)),
            pl.BlockSpec((1, kdim, tn), lambda b, i, j: (layer, 0, j)),
            pl.BlockSpec((1, tm, tn), lambda b, i, j: (b, i, j)),
            pl.BlockSpec((1, 1, tn), lambda b, i, j: (mod_row(b), 0, gate_blk * nblk + j)),
        ],
        out_specs=pl.BlockSpec((1, tm, tn), lambda b, i, j: (b, i, j)),
        out_shape=jax.ShapeDtypeStruct(x.shape, F32),
        compiler_params=_cparams("parallel", "parallel", "parallel"),
        name="outproj_residual",
    )(y, w, x, mod_l)


def _router_kernel(x_ref, g_ref, sh_ref, sc_ref, w_ref, a_ref, hi_ref, lo_ref):
    tm = hi_ref.shape[0]
    slab = _row_slab(tm)
    gv = g_ref[...]
    scale1 = 1.0 + sc_ref[0]
    shift = sh_ref[0]

    def body(r, carry):
        rows = pl.ds(pl.multiple_of(r * slab, slab), slab)
        h = _norm_mod_rows(x_ref[0, rows, :], gv, scale1, shift)
        hi = h.astype(BF16)
        hi_ref[rows, :] = hi
        lo_ref[rows, :] = (h - hi.astype(F32)).astype(BF16)
        return carry

    lax.fori_loop(0, tm // slab, body, 0)
    w = w_ref[...]
    r_hi = _dot(hi_ref[...], w)
    r_lo = _dot(lo_ref[...], w)
    logits = (r_hi + pltpu.roll(r_hi, LANES - N_EXPERTS, 1) + r_lo).T[:N_EXPERTS]
    m = jnp.max(logits, axis=0, keepdims=True)
    e = jnp.exp(logits - m)
    a_ref[0] = e / jnp.sum(e, axis=0, keepdims=True)


def _router(x, norm_g, mod_l, mod_row, w_router):
    bsz, seq, d = x.shape
    tm = min(512, seq)
    return pl.pallas_call(
        _router_kernel,
        grid=(bsz, seq // tm),
        in_specs=[
            pl.BlockSpec((1, tm, d), lambda b, i: (b, i, 0)),
            pl.BlockSpec((1, d), lambda b, i: (0, 0)),
            pl.BlockSpec((1, 1, d), lambda b, i: (mod_row(b), 0, 3)),
            pl.BlockSpec((1, 1, d), lambda b, i: (mod_row(b), 0, 4)),
            pl.BlockSpec((d, LANES), lambda b, i: (0, 0)),
        ],
        out_specs=pl.BlockSpec((1, N_EXPERTS, tm), lambda b, i: (b, 0, i)),
        out_shape=jax.ShapeDtypeStruct((bsz, N_EXPERTS, seq), F32),
        scratch_shapes=[pltpu.VMEM((tm, d), BF16), pltpu.VMEM((tm, d), BF16)],
        compiler_params=_cparams("parallel", "parallel"),
        name="router_affinity",
    )(x, norm_g.reshape(1, d), mod_l, mod_l, w_router)


def _lane_cumsum(src_ref, dst_ref, seq):
    r = lax.broadcasted_iota(I32, (LANES, LANES), 0)
    c = lax.broadcasted_iota(I32, (LANES, LANES), 1)
    upper = (r <= c).astype(BF16)
    carry = jnp.zeros((N_EXPERTS, 1), F32)
    for jb in range(seq // LANES):
        sl = slice(jb * LANES, (jb + 1) * LANES)
        cs = _dot(src_ref[:, sl].astype(BF16), upper) + carry
        dst_ref[:, sl] = cs
        carry = cs[:, LANES - 1:LANES]


def _topk_kernel(a_ref, idx_ref, tok_ref, m_scr, cum_scr, pack_scr, *, cap):
    a = a_ref[0]
    seq = a.shape[1]
    bits = lax.bitcast_convert_type(a, I32)

    def search(i, prefix):
        cand = prefix | jnp.left_shift(jnp.int32(1), 30 - i)
        cnt = jnp.sum((bits >= cand).astype(F32), axis=1, keepdims=True)
        return jnp.where(cnt >= cap, cand, prefix)

    thr = lax.fori_loop(0, 31, search, jnp.zeros((N_EXPERTS, 1), I32))
    gt = bits > thr
    eq = bits == thr
    need = cap - jnp.sum(gt.astype(F32), axis=1, keepdims=True)
    m_scr[...] = eq.astype(F32)
    _lane_cumsum(m_scr, cum_scr, seq)
    sel = gt | (eq & (cum_scr[...] - 1.0 < need))
    m_scr[...] = sel.astype(F32)
    _lane_cumsum(m_scr, cum_scr, seq)

    pack_scr[...] = jnp.zeros_like(pack_scr)
    pack_scr[0:N_EXPERTS, :] = jnp.where(sel, cum_scr[...] - 1.0, -1.0)
    pack_scr[N_EXPERTS:2 * N_EXPERTS, :] = jnp.where(sel, a, 0.0)
    tok_ref[0] = pack_scr[...].T

    cap_pad = idx_ref.shape[2]
    ones = jnp.ones((SUBLANES, LANES), BF16)
    for e in range(N_EXPERTS):
        for pb in range(cap_pad // LANES):
            slot = (lax.broadcasted_iota(I32, (LANES, 1), 0) + pb * LANES).astype(F32)

            def count_block(jb, acc, e=e, slot=slot):
                row = cum_scr[e:e + 1, pl.ds(pl.multiple_of(jb * LANES, LANES), LANES)]
                return acc + (row <= slot).astype(F32)

            acc = lax.fori_loop(0, seq // LANES, count_block, jnp.zeros((LANES, LANES), F32))
            counts = _dot_nt(ones, acc.astype(BF16))
            idx_ref[0, e:e + 1, pb * LANES:(pb + 1) * LANES] = (
                jnp.minimum(counts[0:1, :], seq - 1.0).astype(I32))


def _topk(aff_t, cap, cap_pad):
    bsz, _, seq = aff_t.shape
    return pl.pallas_call(
        functools.partial(_topk_kernel, cap=cap),
        grid=(bsz,),
        in_specs=[pl.BlockSpec((1, N_EXPERTS, seq), lambda b: (b, 0, 0))],
        out_specs=[
            pl.BlockSpec((1, N_EXPERTS, cap_pad), lambda b: (b, 0, 0)),
            pl.BlockSpec((1, seq, LANES), lambda b: (b, 0, 0)),
        ],
        out_shape=[
            jax.ShapeDtypeStruct((bsz, N_EXPERTS, cap_pad), I32),
            jax.ShapeDtypeStruct((bsz, seq, LANES), F32),
        ],
        scratch_shapes=[
            pltpu.VMEM((N_EXPERTS, seq), F32), pltpu.VMEM((N_EXPERTS, seq), F32),
            pltpu.VMEM((LANES, seq), F32),
        ],
        compiler_params=_cparams("parallel"),
        name="expert_choice_select",
    )(aff_t)


def _expert_kernel(idx_ref, x_hbm, g_ref, sh_ref, sc_ref, wg_ref, wu_ref, o_ref, xs_ref, hs_ref, sem, *,
                   cap, cap_pad):
    nb = pl.num_programs(1)
    e = pl.program_id(0)
    b = pl.program_id(1)
    step = e * nb + b
    cur = step % 2
    has_next = step + 1 < pl.num_programs(0) * nb
    next_e = (step + 1) // nb
    next_b = (step + 1) % nb
    slab = _row_slab(cap_pad)
    n_slabs = cap_pad // slab
    assert cap % n_slabs == 0
    per_slab = cap // n_slabs

    def row_copy(buf, bb, p, t):
        return pltpu.make_async_copy(x_hbm.at[bb, pl.ds(t, 1), :], xs_ref.at[buf, pl.ds(p, 1), :], sem.at[buf])

    @pl.when(step == 0)
    def _():
        if cap < cap_pad:
            xs_ref[:, cap:cap_pad, :] = jnp.zeros((2, cap_pad - cap, xs_ref.shape[2]), F32)
        base = (b * N_EXPERTS + e) * cap_pad

        def issue(p, carry):
            row_copy(0, b, p, idx_ref[base + p]).start()
            return carry

        lax.fori_loop(0, cap, issue, 0, unroll=8)

    def drain(p, carry):
        row_copy(cur, b, p, 0).wait()
        return carry

    lax.fori_loop(0, cap, drain, 0, unroll=8)

    gv = g_ref[...]
    scale1 = 1.0 + sc_ref[0]
    shift = sh_ref[0]
    next_base = (next_b * N_EXPERTS + next_e) * cap_pad

    def norm_loop(start_next):
        def body(r, carry):
            if start_next:
                for q in range(per_slab):
                    p = r * per_slab + q
                    row_copy(1 - cur, next_b, p, idx_ref[next_base + p]).start()
            rows = pl.ds(pl.multiple_of(r * slab, slab), slab)
            hs_ref[rows, :] = _norm_mod_rows(xs_ref[cur, rows, :], gv, scale1, shift).astype(BF16)
            return carry

        lax.fori_loop(0, n_slabs, body, 0)

    @pl.when(has_next)
    def _():
        norm_loop(True)

    @pl.when(jnp.logical_not(has_next))
    def _():
        norm_loop(False)

    hs = hs_ref[...]
    o_ref[0, 0] = (_silu(_dot(hs, wg_ref[0, 0])) * _dot(hs, wu_ref[0, 0])).astype(o_ref.dtype)


def _experts(idx, cap, x, norm_g, mod_l, mod_row, w_gate, w_up, layer):
    bsz, seq, d = x.shape
    cap_pad = idx.shape[2]
    grid_spec = pltpu.PrefetchScalarGridSpec(
        num_scalar_prefetch=1,
        grid=(N_EXPERTS, bsz),
        in_specs=[
            pl.BlockSpec(memory_space=pl.ANY),
            pl.BlockSpec((1, d), lambda e, b, idx_ref: (0, 0)),
            pl.BlockSpec((1, 1, d), lambda e, b, idx_ref: (mod_row(b), 0, 3)),
            pl.BlockSpec((1, 1, d), lambda e, b, idx_ref: (mod_row(b), 0, 4)),
            pl.BlockSpec((1, 1, d, EXPERT_FF), lambda e, b, idx_ref: (layer, e, 0, 0)),
            pl.BlockSpec((1, 1, d, EXPERT_FF), lambda e, b, idx_ref: (layer, e, 0, 0)),
        ],
        out_specs=pl.BlockSpec((1, 1, cap_pad, EXPERT_FF), lambda e, b, idx_ref: (b, e, 0, 0)),
        scratch_shapes=[pltpu.VMEM((2, cap_pad, d), F32), pltpu.VMEM((cap_pad, d), BF16),
                        pltpu.SemaphoreType.DMA((2,))],
    )
    return pl.pallas_call(
        functools.partial(_expert_kernel, cap=cap, cap_pad=cap_pad),
        grid_spec=grid_spec,
        out_shape=jax.ShapeDtypeStruct((bsz, N_EXPERTS, cap_pad, EXPERT_FF), BF16),
        compiler_params=_cparams("arbitrary", "arbitrary"),
        name="expert_gather_swiglu",
    )(idx.reshape(-1), x, norm_g.reshape(1, d), mod_l, mod_l, w_gate, w_up)


def _down_kernel(tok_ref, hid_ref, wd_ref, x_ref, g_ref, o_ref, hcat_ref):
    tm = hcat_ref.shape[0]
    cap_pad = hid_ref.shape[2]

    @pl.when(pl.program_id(2) == 0)
    def _():
        tok = tok_ref[0]
        slot = lax.broadcasted_iota(I32, (tm, cap_pad), 1).astype(F32)
        for e in range(N_EXPERTS):
            onehot = (tok[:, e:e + 1] == slot).astype(BF16)
            rows = _dot(onehot, hid_ref[0, e])
            hcat_ref[:, e * EXPERT_FF:(e + 1) * EXPERT_FF] = (
                rows * tok[:, N_EXPERTS + e:N_EXPERTS + e + 1]).astype(BF16)

    o_ref[0] = x_ref[0] + g_ref[0] * _dot(hcat_ref[...], wd_ref[0])


def _combine_down(tok, hid, w_down_cat, layer, x, mod_l, mod_row):
    bsz, seq, d = x.shape
    cap_pad = hid.shape[2]
    kdim = N_EXPERTS * EXPERT_FF
    tm = min(512, seq)
    tn = min(1024, d)
    nblk = d // tn
    return pl.pallas_call(
        _down_kernel,
        grid=(bsz, seq // tm, nblk),
        in_specs=[
            pl.BlockSpec((1, tm, LANES), lambda b, i, j: (b, i, 0)),
            pl.BlockSpec((1, N_EXPERTS, cap_pad, EXPERT_FF), lambda b, i, j: (b, 0, 0, 0)),
            pl.BlockSpec((1, kdim, tn), lambda b, i, j: (layer, 0, j)),
            pl.BlockSpec((1, tm, tn), lambda b, i, j: (b, i, j)),
            pl.BlockSpec((1, 1, tn), lambda b, i, j: (mod_row(b), 0, 5 * nblk + j)),
        ],
        out_specs=pl.BlockSpec((1, tm, tn), lambda b, i, j: (b, i, j)),
        out_shape=jax.ShapeDtypeStruct(x.shape, F32),
        scratch_shapes=[pltpu.VMEM((tm, kdim), BF16)],
        compiler_params=_cparams("parallel", "parallel", "arbitrary"),
        name="combine_down_residual",
    )(tok, hid, w_down_cat, x, mod_l)


def _ec_ffn_residual(x, norm_g, mod_l, mod_row, w_router, w_gate, w_up, w_down_cat, layer):
    seq = x.shape[1]
    cap = CAPACITY_FACTOR * seq // N_EXPERTS
    cap_pad = -(-cap // LANES) * LANES
    aff_t = _router(x, norm_g, mod_l, mod_row, w_router)
    idx, tok = _topk(aff_t, cap, cap_pad)
    hid = _experts(idx, cap, x, norm_g, mod_l, mod_row, w_gate, w_up, layer)
    return _combine_down(tok, hid, w_down_cat, layer, x, mod_l, mod_row)


def _final_kernel(x_ref, g_ref, o_ref):
    tm = x_ref.shape[1]
    slab = _row_slab(tm)
    gv = g_ref[...]

    def body(r, carry):
        rows = pl.ds(pl.multiple_of(r * slab, slab), slab)
        x = x_ref[0, rows, :]
        ms = jnp.mean(x * x, axis=-1, keepdims=True)
        o_ref[0, rows, :] = x * lax.rsqrt(ms + EPS) * gv
        return carry

    lax.fori_loop(0, tm // slab, body, 0)


def _final_norm(x, g):
    bsz, seq, d = x.shape
    tm = min(512, seq)
    return pl.pallas_call(
        _final_kernel,
        grid=(bsz, seq // tm),
        in_specs=[
            pl.BlockSpec((1, tm, d), lambda b, i: (b, i, 0)),
            pl.BlockSpec((1, d), lambda b, i: (0, 0)),
        ],
        out_specs=pl.BlockSpec((1, tm, d), lambda b, i: (b, i, 0)),
        out_shape=jax.ShapeDtypeStruct(x.shape, F32),
        compiler_params=_cparams("parallel", "parallel"),
        name="final_rmsnorm",
    )(x, g.reshape(1, d))


def _split_inproj_weight(w_in):
    w_main = w_in[:, :, :N_MAIN].astype(BF16)
    w_small = w_in[:, :, N_MAIN:]
    w_small = jnp.pad(w_small, ((0, 0), (0, 0), (0, N_SMALL - w_small.shape[2]))).astype(BF16)
    return w_main, w_small


def kernel(x, c, ctx, c_ctx, ada_w, ada_b, norm1_g, norm2_g, gla_w_in, gla_wg2, gla_bg2, gla_onorm, gla_w_out, gdn_w_in, gdn_conv, gdn_A_log, gdn_dt_bias, gdn_onorm, gdn_w_out, router_w, exp_w_gate, exp_w_up, exp_w_down, final_g):
    bsz, _, d = x.shape
    depth = ada_w.shape[0]
    n_rows = -(-(bsz + 1) // SUBLANES) * SUBLANES
    c_rows = jnp.zeros((n_rows, d), F32).at[:bsz].set(c).at[bsz].set(c_ctx)
    mod_l = _modulation(c_rows, ada_w, ada_b).reshape(depth * n_rows, 1, 6 * d)

    w_in = (_split_inproj_weight(gla_w_in), _split_inproj_weight(gdn_w_in))
    w_outs = (gla_w_out.astype(BF16), gdn_w_out.astype(BF16))
    w_gate = exp_w_gate.astype(BF16)
    w_up = exp_w_up.astype(BF16)
    w_down_cat = exp_w_down.reshape(depth, N_EXPERTS * EXPERT_FF, d).astype(BF16)

    cx = ctx
    for i in range(depth):
        last = i == depth - 1
        j = i // 2

        def lat_row(b, i=i):
            return i * n_rows + b

        def ctx_row(b, i=i):
            return i * n_rows + bsz

        w_main, w_small = w_in[i % 2]
        w_out = w_outs[i % 2]
        pc, smc = _inproj(cx, norm1_g[i], mod_l, ctx_row, w_main, w_small, j)
        px, smx = _inproj(x, norm1_g[i], mod_l, lat_row, w_main, w_small, j)
        if i % 2 == 0:
            wg_pad = jnp.zeros((2, N_SMALL, HK), F32)
            wg_pad = wg_pad.at[0, :GLA_LOWRANK].set(gla_wg2[j, 0])
            wg_pad = wg_pad.at[1, GLA_LOWRANK:2 * GLA_LOWRANK].set(gla_wg2[j, 1]).astype(BF16)
            yc, yx = _gla_mix(pc, smc, px, smx, wg_pad, gla_bg2[j].reshape(2, 1, HK), gla_onorm[j])
        else:
            yc, yx = _gdn_mix(pc, smc, px, smx, gdn_conv[j], gdn_A_log[j], gdn_dt_bias[j], gdn_onorm[j])
        w_r_hi = router_w[i].astype(BF16)
        w_r_lo = (router_w[i] - w_r_hi.astype(F32)).astype(BF16)
        w_router = jnp.pad(jnp.concatenate([w_r_hi, w_r_lo], axis=1), ((0, 0), (0, LANES - 2 * N_EXPERTS)))
        x = _outproj(yx, w_out, j, x, mod_l, lat_row, 2)
        x = _ec_ffn_residual(x, norm2_g[i], mod_l, lat_row, w_router, w_gate, w_up, w_down_cat, i)
        if not last:
            cx = _outproj(yc, w_out, j, cx, mod_l, ctx_row, 2)
            cx = _ec_ffn_residual(cx, norm2_g[i], mod_l, ctx_row, w_router, w_gate, w_up, w_down_cat, i)
    return _final_norm(x, final_g)
```
